```python
import jax, jax.numpy as jnp
from jax import lax
import numpy as np

D_MODEL = 1024
BATCH = 16
SEQ = 256
DEPTH = 1
DEC_BATCH = 4
DEC_SEQ = 4096
PAST_LEN = 256

GRID_W = 64
N_HEADS = 16
HEAD_DIM = 64
D_RWKV = N_HEADS * HEAD_DIM
D_CONV = D_MODEL
CONV_W = 3
D_DECAY_LORA = 64
D_AAA_LORA = 64
D_GATE_LORA = 128
D_FF = 4 * D_MODEL
N_BRANCH = 2
D_IN_PROJ = 3 * D_RWKV + 2 * D_DECAY_LORA + 2 * D_AAA_LORA + D_GATE_LORA + 3 * D_CONV + N_BRANCH * D_MODEL
EPS = 1e-6
LNX_EPS = 64e-5

kernel_name = 'bidir_rwkv7_shortconv_flow_step'


def _in_split_points():
    sizes = (D_RWKV, D_RWKV, D_RWKV, 2 * D_DECAY_LORA, 2 * D_AAA_LORA, D_GATE_LORA, D_CONV, D_CONV, D_CONV)
    pts, acc = [], 0
    for s in sizes:
        acc += s
        pts.append(acc)
    return pts


def rms_norm(x, g):
    xf = x.astype(jnp.float32)
    y = xf * lax.rsqrt(jnp.mean(xf * xf, axis=-1, keepdims=True) + EPS)
    return (y * g.astype(jnp.float32)).astype(x.dtype)


def ada_modulation(cvec, w_ada, b_ada):
    m = jax.nn.silu(cvec) @ w_ada + b_ada
    return jnp.split(m[..., None, :], 6, axis=-1)


def centred_conv3(u, w):
    L = u.shape[-2]
    pad = [(0, 0)] * (u.ndim - 2) + [(1, 1), (0, 0)]
    up = jnp.pad(u, pad)
    return w[0] * up[..., 0:L, :] + w[1] * up[..., 1:L + 1, :] + w[2] * up[..., 2:L + 2, :]


def to_heads(u):
    return u.reshape(u.shape[:-1] + (N_HEADS, HEAD_DIM))


def wkv_scan(s0, r, w, k, v, a, b, reverse):
    def seq_major(u):
        return jnp.moveaxis(to_heads(u.astype(jnp.float32)), 1, 0)

    def step(S, inp):
        r_t, w_t, k_t, v_t, a_t, b_t = inp
        sa = jnp.einsum('bhvk,bhk->bhv', S, a_t)
        S = S * w_t[:, :, None, :] + sa[..., None] * b_t[:, :, None, :] + v_t[..., None] * k_t[:, :, None, :]
        return S, jnp.einsum('bhvk,bhk->bhv', S, r_t)

    s_final, y = lax.scan(step, s0.astype(jnp.float32),
                          (seq_major(r), seq_major(w), seq_major(k), seq_major(v), seq_major(a), seq_major(b)),
                          reverse=reverse)
    return jnp.moveaxis(y, 0, 1), s_final.astype(s0.dtype)


def token_mixer(xn, s0_f, s0_b, p, on_grid):
    B, T, _ = xn.shape
    z = xn @ p['w_in']
    r, k, v, wl, al, gl, cb, cc, ch, zg = jnp.split(z, _in_split_points(), axis=-1)

    g_out = jax.nn.sigmoid(gl) @ p['g2']
    kk = to_heads((k * p['k_k']).astype(jnp.float32))
    kk = kk * lax.rsqrt(jnp.maximum(jnp.sum(kk * kk, axis=-1, keepdims=True), 1e-12))
    kk = kk.reshape(B, T, D_RWKV)
    rh = to_heads(r.astype(jnp.float32))
    vh = to_heads(v.astype(jnp.float32))
    ys, bonuses, states = [], [], []
    for d, (s0, reverse) in enumerate(((s0_f, False), (s0_b, True))):
        wl_d = wl[..., d * D_DECAY_LORA:(d + 1) * D_DECAY_LORA]
        al_d = al[..., d * D_AAA_LORA:(d + 1) * D_AAA_LORA]
        w_log = -jax.nn.softplus(-(p['w0'][d] + jnp.tanh(wl_d) @ p['w2'][d])) - 0.5
        decay = jnp.exp(-jnp.exp(w_log.astype(jnp.float32)))
        a = jax.nn.sigmoid(p['a0'][d] + al_d @ p['a2'][d])
        kd = k * (1.0 + (a - 1.0) * p['k_a'])
        y_d, s_d = wkv_scan(s0, r, decay, kd, v, -kk, kk * a, reverse)
        ys.append(y_d)
        bonuses.append(jnp.sum(rh * to_heads(kd.astype(jnp.float32)) * p['r_k'], axis=-1, keepdims=True) * vh)
        states.append(s_d)
    wkv = ys[0] + ys[1]
    mu = jnp.mean(wkv, axis=-1, keepdims=True)
    var = jnp.mean(jnp.square(wkv - mu), axis=-1, keepdims=True)
    ln = ((wkv - mu) * lax.rsqrt(var + LNX_EPS)).reshape(B, T, D_RWKV) * p['lnx_w'] + p['lnx_b']
    x_a = (ln + (bonuses[0] + bonuses[1]).reshape(B, T, D_RWKV)).astype(xn.dtype) * g_out
    y_a = x_a @ p['w_pa']

    u = cc * ch
    if on_grid:
        rows = T // GRID_W
        conv_u = centred_conv3(u.reshape(B, rows, GRID_W, D_CONV), p['conv_w']).reshape(B, T, D_CONV)
    else:
        conv_u = centred_conv3(u, p['conv_w'])
    y_b = (cb * conv_u) @ p['w_pb']

    gate_a, gate_b = jnp.split(jax.nn.sigmoid(zg), 2, axis=-1)
    out = (gate_a * y_a + gate_b * y_b) @ p['w_o']
    return out, states[0], states[1]


def layer_apply(x, cvec, s0_f, s0_b, p, on_grid):
    sh1, sc1, gt1, sh2, sc2, gt2 = ada_modulation(cvec, p['w_ada'], p['b_ada'])
    xn = rms_norm(x, p['norm1_g']) * (1.0 + sc1) + sh1
    mix, s_f, s_b = token_mixer(xn, s0_f, s0_b, p, on_grid)
    x = x + gt1 * mix
    xn = rms_norm(x, p['norm2_g']) * (1.0 + sc2) + sh2
    x = x + gt2 * (jnp.square(jax.nn.relu(xn @ p['w_ff1'])) @ p['w_ff2'])
    return x, s_f, s_b


def setup_inputs(seed: int = 0) -> dict:
    key = jax.random.key(seed)
    ks = jax.random.split(key, 32)

    def nrm(i, shape, scale):
        return scale * jax.random.normal(ks[i], shape, dtype=jnp.float32)

    st_shape = (DEC_BATCH, DEPTH, N_HEADS, HEAD_DIM, HEAD_DIM)
    return {
        'x_prompt': nrm(0, (BATCH, SEQ, D_MODEL), 1.0),
        'x_sample': nrm(1, (DEC_BATCH, DEC_SEQ, D_MODEL), 1.0),
        'state_rwkv_fwd': nrm(2, st_shape, 0.5),
        'state_rwkv_bwd': nrm(3, st_shape, 0.5),
        'c': nrm(4, (DEC_BATCH, D_MODEL), 1.0),
        'c_ctx': nrm(5, (D_MODEL,), 1.0),
        'norm1_g': 1.0 + nrm(6, (DEPTH, D_MODEL), 0.05),
        'norm2_g': 1.0 + nrm(7, (DEPTH, D_MODEL), 0.05),
        'w_ada': nrm(8, (DEPTH, D_MODEL, 6 * D_MODEL), 0.5 * D_MODEL ** -0.5),
        'b_ada': nrm(9, (DEPTH, 6 * D_MODEL), 0.02),
        'w_in': nrm(10, (DEPTH, D_MODEL, D_IN_PROJ), D_MODEL ** -0.5),
        'w0': -1.0 + nrm(11, (DEPTH, 2, D_RWKV), 0.3),
        'w2': nrm(12, (DEPTH, 2, D_DECAY_LORA, D_RWKV), 0.1),
        'a0': nrm(13, (DEPTH, 2, D_RWKV), 0.1),
        'a2': nrm(14, (DEPTH, 2, D_AAA_LORA, D_RWKV), 0.5 * D_AAA_LORA ** -0.5),
        'g2': nrm(15, (DEPTH, D_GATE_LORA, D_RWKV), D_GATE_LORA ** -0.5),
        'k_k': 0.85 + nrm(16, (DEPTH, D_RWKV), 0.05),
        'k_a': 1.0 + nrm(17, (DEPTH, D_RWKV), 0.05),
        'r_k': nrm(18, (DEPTH, N_HEADS, HEAD_DIM), 0.1),
        'lnx_w': 1.0 + nrm(19, (DEPTH, D_RWKV), 0.05),
        'lnx_b': nrm(20, (DEPTH, D_RWKV), 0.02),
        'conv_w': nrm(21, (DEPTH, CONV_W, D_CONV), CONV_W ** -0.5),
        'w_pa': nrm(22, (DEPTH, D_RWKV, D_MODEL), D_RWKV ** -0.5),
        'w_pb': nrm(23, (DEPTH, D_CONV, D_MODEL), D_CONV ** -0.5),
        'w_o': nrm(24, (DEPTH, D_MODEL, D_MODEL), D_MODEL ** -0.5),
        'w_ff1': nrm(25, (DEPTH, D_MODEL, D_FF), D_MODEL ** -0.5),
        'w_ff2': nrm(26, (DEPTH, D_FF, D_MODEL), D_FF ** -0.5),
        'final_norm_g': 1.0 + nrm(27, (D_MODEL,), 0.05),
    }


def reference(x_prompt, x_sample, state_rwkv_fwd, state_rwkv_bwd, c, c_ctx,
              norm1_g, norm2_g, w_ada, b_ada, w_in, w0, w2, a0, a2, g2, k_k, k_a, r_k,
              lnx_w, lnx_b, conv_w, w_pa, w_pb, w_o, w_ff1, w_ff2, final_norm_g):
    stacked = dict(norm1_g=norm1_g, norm2_g=norm2_g, w_ada=w_ada, b_ada=b_ada, w_in=w_in,
                   w0=w0, w2=w2, a0=a0, a2=a2, g2=g2, k_k=k_k, k_a=k_a, r_k=r_k,
                   lnx_w=lnx_w, lnx_b=lnx_b, conv_w=conv_w, w_pa=w_pa, w_pb=w_pb, w_o=w_o,
                   w_ff1=w_ff1, w_ff2=w_ff2)

    h = x_prompt
    zero_state = jnp.zeros((x_prompt.shape[0], N_HEADS, HEAD_DIM, HEAD_DIM), x_prompt.dtype)
    fwd_states, bwd_states = [], []
    for l in range(DEPTH):
        p = {name: arr[l] for name, arr in stacked.items()}
        h, s_f, s_b = layer_apply(h, c_ctx, zero_state, zero_state, p, False)
        fwd_states.append(s_f)
        bwd_states.append(s_b)
    y_prompt = rms_norm(h, final_norm_g)
    new_state_fwd = jnp.stack(fwd_states, axis=1)
    new_state_bwd = jnp.stack(bwd_states, axis=1)

    g = x_sample
    for l in range(DEPTH):
        p = {name: arr[l] for name, arr in stacked.items()}
        g, _, _ = layer_apply(g, c, state_rwkv_fwd[:, l], state_rwkv_bwd[:, l], p, True)
    y_sample = rms_norm(g, final_norm_g)

    return (y_prompt, y_sample, new_state_fwd, new_state_bwd)
```

```python
import functools

import jax
import jax.numpy as jnp
from jax import lax
from jax.experimental import pallas as pl
from jax.experimental.pallas import tpu as pltpu

F32 = jnp.float32
BF16 = jnp.bfloat16

HEAD_DIM = 64
LANES = 128
PAIR = LANES // HEAD_DIM
GRID_W = 64
EPS = 1e-6
LNX_EPS = 64e-5
CHUNK = 64
TOKEN_TILE = 256
VMEM_LIMIT = 58 * 1024 * 1024


def _resident(shape):
    nd = len(shape)
    return pl.BlockSpec(shape, lambda *_: (0,) * nd, pipeline_mode=pl.Buffered(1))


def _dot(a, b):
    return jnp.dot(a, b, preferred_element_type=F32)


def _softplus(x):
    return jnp.maximum(x, 0.0) + jnp.log1p(jnp.exp(-jnp.abs(x)))


def _head_sum(x, m0):
    lo = jnp.sum(jnp.where(m0, x, 0.0), axis=-1, keepdims=True)
    hi = jnp.sum(jnp.where(m0, 0.0, x), axis=-1, keepdims=True)
    return jnp.where(m0, lo, hi)


def _rms(x):
    return x * lax.rsqrt(jnp.mean(x * x, axis=-1, keepdims=True) + EPS)


def _ada_kernel(c_ref, w_ref, b_ref, o_ref):
    c = c_ref[...]
    s = (c * jax.nn.sigmoid(c)).astype(BF16)
    o_ref[...] = _dot(s, w_ref[...].astype(BF16)) + b_ref[...]


def _ada(cv, w_ada, b_ada):
    rows, d = cv.shape
    n = w_ada.shape[1]
    tn = d
    return pl.pallas_call(
        _ada_kernel,
        grid=(n // tn,),
        in_specs=[
            pl.BlockSpec((rows, d), lambda j: (0, 0)),
            pl.BlockSpec((d, tn), lambda j: (0, j)),
            pl.BlockSpec((1, tn), lambda j: (0, j)),
        ],
        out_specs=pl.BlockSpec((rows, tn), lambda j: (0, j)),
        out_shape=jax.ShapeDtypeStruct((rows, n), F32),
        name="ada_mod",
    )(cv, w_ada, b_ada)


def _in_proj_kernel(seg, d, x_ref, mod_ref, g1_ref, win_ref, w2_ref, w0_ref, a2_ref, a0_ref, g2_ref,
                    cw_ref, wpb_ref, ka_ref, rk_ref,
                    r_ref, k_ref, v_ref, a_ref, lw_ref, gout_ref, bonus_ref, ga_ref, gbyb_ref):
    tm = x_ref.shape[0]
    npair = d // LANES
    mod = mod_ref[0]
    sh1 = mod[0:1]
    sc1 = mod[1:2]
    xn = _rms(x_ref[...]) * g1_ref[...]
    xn = (xn * (1.0 + sc1) + sh1).astype(BF16)

    def proj(lo, hi):
        return _dot(xn, win_ref[:, lo:hi])

    o_lora = 3 * d
    zl = proj(o_lora, o_lora + 3 * LANES)
    wl = zl[:, 0:LANES]
    al = zl[:, LANES:2 * LANES]
    gl = zl[:, 2 * LANES:3 * LANES]
    pre_w = _dot(jnp.tanh(wl).astype(BF16), w2_ref[...]) + w0_ref[...]
    lw = -jnp.exp(-_softplus(-pre_w) - 0.5)
    a = jax.nn.sigmoid(_dot(al.astype(BF16), a2_ref[...]) + a0_ref[...])
    gout_ref[...] = _dot(jax.nn.sigmoid(gl).astype(BF16), g2_ref[...])

    r = proj(0, d)
    k = proj(d, 2 * d)
    v = proj(2 * d, 3 * d)
    q = r * rk_ref[...] * k * (2.0 + (a[:, :d] + a[:, d:] - 2.0) * ka_ref[...])
    m0 = lax.broadcasted_iota(jnp.int32, (1, LANES), 1) < HEAD_DIM
    for p in range(npair):
        sl = slice(p * LANES, (p + 1) * LANES)
        r_ref[p] = r[:, sl]
        k_ref[p] = k[:, sl]
        v_ref[p] = v[:, sl]
        for dr in range(2):
            sd = slice(dr * d + p * LANES, dr * d + (p + 1) * LANES)
            a_ref[dr, p] = a[:, sd]
            lw_ref[dr, p] = lw[:, sd]
        bonus_ref[:, sl] = v[:, sl] * _head_sum(q[:, sl], m0)

    o_conv = o_lora + 3 * LANES
    cb = proj(o_conv, o_conv + d)
    u = proj(o_conv + d, o_conv + 2 * d) * proj(o_conv + 2 * d, o_conv + 3 * d)
    pos = lax.broadcasted_iota(jnp.int32, (tm, 1), 0) % seg
    u_prev = jnp.where(pos == 0, 0.0, pltpu.roll(u, 1, 0))
    u_next = jnp.where(pos == seg - 1, 0.0, pltpu.roll(u, tm - 1, 0))
    conv = cw_ref[0:1] * u_prev + cw_ref[1:2] * u + cw_ref[2:3] * u_next
    yb = _dot((cb * conv).astype(BF16), wpb_ref[...])

    o_gate = o_conv + 3 * d
    ga_ref[...] = jax.nn.sigmoid(proj(o_gate, o_gate + d))
    gbyb_ref[...] = jax.nn.sigmoid(proj(o_gate + d, o_gate + 2 * d)) * yb


def _in_proj(x, mod, tokens_per_mod, seg, wts):
    ntok, d = x.shape
    tm = TOKEN_TILE
    npair = d // LANES
    flat = jax.ShapeDtypeStruct((ntok, d), F32)
    pm = jax.ShapeDtypeStruct((npair, ntok, LANES), F32)
    pm2 = jax.ShapeDtypeStruct((2, npair, ntok, LANES), F32)
    flat_spec = pl.BlockSpec((tm, d), lambda i: (i, 0))
    pm_spec = pl.BlockSpec((npair, tm, LANES), lambda i: (0, i, 0))
    pm2_spec = pl.BlockSpec((2, npair, tm, LANES), lambda i: (0, 0, i, 0))
    names = ("g1", "w_in", "w2cat", "w0cat", "a2cat", "a0cat", "g2", "conv_w", "w_pb", "k_a", "r_k")
    return pl.pallas_call(
        functools.partial(_in_proj_kernel, seg, d),
        grid=(ntok // tm,),
        in_specs=[flat_spec,
                  pl.BlockSpec((1,) + mod.shape[1:], lambda i: ((i * tm) // tokens_per_mod, 0, 0))]
                 + [_resident(wts[n].shape) for n in names],
        out_specs=[pm_spec, pm_spec, pm_spec, pm2_spec, pm2_spec, flat_spec, flat_spec, flat_spec, flat_spec],
        out_shape=[pm, pm, pm, pm2, pm2, flat, flat, flat, flat],
        compiler_params=pltpu.CompilerParams(dimension_semantics=("parallel",), vmem_limit_bytes=VMEM_LIMIT),
        name="in_proj",
    )(x, mod, *[wts[n] for n in names])


def _scan_kernel(n_chunks, r_ref, k_ref, v_ref, a_ref, lw_ref, kkw_ref, ka_ref, s0_ref,
                 y_ref, sfin_ref, h_ref):
    dr = pl.program_id(1)
    c = pl.program_id(2)
    npair = r_ref.shape[0]
    cl = r_ref.shape[1]
    n2 = PAIR * cl

    @pl.when(c == 0)
    def _():
        h_ref[...] = s0_ref[0, 0]

    sgn = 1 - 2 * dr
    row = lax.broadcasted_iota(jnp.int32, (n2, n2), 0)
    col = lax.broadcasted_iota(jnp.int32, (n2, n2), 1)
    same = (row // cl) == (col // cl)
    dt = ((row % cl) - (col % cl)) * sgn
    mask_s = same & (dt > 0)
    mask_i = same & (dt >= 0)
    eye = (row == col).astype(F32)
    tr = lax.broadcasted_iota(jnp.int32, (cl, cl), 0)
    tc = lax.broadcasted_iota(jnp.int32, (cl, cl), 1)
    tri = (((tr - tc) * sgn) >= 0).astype(BF16)
    m0 = lax.broadcasted_iota(jnp.int32, (1, LANES), 1) < HEAD_DIM

    def stack(x):
        return jnp.concatenate([jnp.where(m0, x, 0.0), jnp.where(m0, 0.0, x)], axis=0)

    def pair_body(p, carry):
        r = r_ref[p]
        k = k_ref[p]
        v = v_ref[p]
        a = a_ref[0, p]
        lw = lw_ref[0, p]
        kkr = k * kkw_ref[p]
        kk = kkr * lax.rsqrt(jnp.maximum(_head_sum(kkr * kkr, m0), 1e-12))
        kd = k * (1.0 + (a - 1.0) * ka_ref[p])
        bv = kk * a

        lw_hi = lw.astype(BF16)
        rem = lw - lw_hi.astype(F32)
        lw_mid = rem.astype(BF16)
        lw_lo = (rem - lw_mid.astype(F32)).astype(BF16)
        cs3 = _dot(tri, jnp.concatenate([lw_hi, lw_mid, lw_lo], axis=1))
        cs = cs3[:, :LANES] + cs3[:, LANES:2 * LANES] + cs3[:, 2 * LANES:]
        tot = jnp.sum(lw, axis=0, keepdims=True)

        e_neg = jnp.exp(-cs)
        e_rem = jnp.exp(tot - cs)
        rt = r * jnp.exp(cs)
        at = -kk * jnp.exp(cs - lw)
        kt = kd * e_neg
        bt = bv * e_neg
        kh = kd * e_rem
        bh = bv * e_rem

        rt_st = stack(rt)
        at_st = stack(at).astype(BF16)
        lhs = jnp.concatenate([at_st, rt_st.astype(BF16)], axis=0)
        btb = bt.astype(BF16)
        ktb = kt.astype(BF16)
        rhs = jnp.concatenate([btb, btb, ktb, ktb], axis=0)
        o1 = lax.dot_general(lhs, rhs, (((1,), (1,)), ((), ())), preferred_element_type=F32)
        a_ab = jnp.where(mask_s, o1[:n2, :n2], 0.0)
        a_ak = jnp.where(mask_s, o1[:n2, n2:], 0.0)
        a_rb = jnp.where(mask_i, o1[n2:, :n2], 0.0)
        a_rk = jnp.where(mask_i, o1[n2:, n2:], 0.0)

        t_inv = eye + a_ab
        npow = a_ab.astype(BF16)
        for _ in range(cl.bit_length() - 2):
            npow = _dot(npow, npow).astype(BF16)
            t_inv = t_inv + _dot(npow, t_inv.astype(BF16))

        v_st = stack(v).astype(BF16)
        x1 = _dot(a_ak.astype(BF16), v_st)
        ut = _dot(t_inv.astype(BF16), jnp.concatenate([x1.astype(BF16), at_st], axis=1))
        rhs2 = jnp.concatenate(
            [jnp.concatenate([v_st, jnp.zeros_like(v_st)], axis=1), ut.astype(BF16)], axis=0)
        o2a = _dot(jnp.concatenate([a_rk, a_rb], axis=1).astype(BF16), rhs2)
        y_loc = o2a[:, :n2]
        q_eff = rt_st + o2a[:, n2:]
        khs = jnp.concatenate([stack(kh), stack(bh)], axis=0).astype(BF16)
        o2b = lax.dot_general(khs, rhs2, (((0,), (0,)), ((), ())), preferred_element_type=F32)
        g = o2b[:, :n2]
        m_rest = o2b[:, n2:]

        h = h_ref[p]
        hb = h.astype(BF16)
        y_st = y_loc + _dot(q_eff.astype(BF16), hb)
        y_ref[0, p] = y_st[:cl] + y_st[cl:]
        gam_col = jnp.transpose(jnp.broadcast_to(jnp.exp(tot), (LANES, LANES)))
        h_ref[p] = gam_col * h + _dot(m_rest.astype(BF16), hb) + g
        return carry

    lax.fori_loop(0, npair, pair_body, 0)

    @pl.when(c == n_chunks - 1)
    def _():
        sfin_ref[0, 0] = h_ref[...]


def _scan(r, k, v, a, lw, kkw, ka, s0, batch):
    npair, ntok, _ = r.shape
    cl = CHUNK
    n_chunks = ntok // batch // cl

    def tok_idx(b, dr, c):
        return b * n_chunks + c + dr * (n_chunks - 1 - 2 * c)

    shared = pl.BlockSpec((npair, cl, LANES), lambda b, dr, c: (0, tok_idx(b, dr, c), 0))
    per_dir = pl.BlockSpec((1, npair, cl, LANES), lambda b, dr, c: (dr, 0, tok_idx(b, dr, c), 0))
    state = pl.BlockSpec((1, 1, npair, LANES, LANES), lambda b, dr, c: (dr, b, 0, 0, 0))
    return pl.pallas_call(
        functools.partial(_scan_kernel, n_chunks),
        grid=(batch, 2, n_chunks),
        in_specs=[shared, shared, shared, per_dir, per_dir,
                  _resident(kkw.shape), _resident(ka.shape), state],
        out_specs=[per_dir, state],
        out_shape=[jax.ShapeDtypeStruct((2, npair, ntok, LANES), F32),
                   jax.ShapeDtypeStruct(s0.shape, F32)],
        scratch_shapes=[pltpu.VMEM((npair, LANES, LANES), F32)],
        compiler_params=pltpu.CompilerParams(
            dimension_semantics=("parallel", "arbitrary", "arbitrary"), vmem_limit_bytes=VMEM_LIMIT),
        name="wkv_scan",
    )(r, k, v, a, lw, kkw, ka, s0)


def _post_kernel(d, y_ref, gout_ref, bonus_ref, ga_ref, gbyb_ref, x_ref, mod_ref, lnw_ref, lnb_ref,
                 wpa_ref, wo_ref, g2n_ref, wff1_ref, wff2_ref, gfin_ref, o_ref, xa_ref):
    npair = d // LANES
    mod = mod_ref[0]
    gt1 = mod[2:3]
    sh2 = mod[3:4]
    sc2 = mod[4:5]
    gt2 = mod[5:6]
    m0 = lax.broadcasted_iota(jnp.int32, (1, LANES), 1) < HEAD_DIM
    inv_n = 1.0 / HEAD_DIM
    for p in range(npair):
        sl = slice(p * LANES, (p + 1) * LANES)
        wkv = y_ref[0, p] + y_ref[1, p]
        cen = wkv - _head_sum(wkv, m0) * inv_n
        var = _head_sum(cen * cen, m0) * inv_n
        ln = cen * lax.rsqrt(var + LNX_EPS) * lnw_ref[:, sl] + lnb_ref[:, sl]
        xa_ref[:, sl] = ((ln + bonus_ref[:, sl]) * gout_ref[:, sl]).astype(BF16)
    ya = _dot(xa_ref[...], wpa_ref[...])
    merged = ga_ref[...] * ya + gbyb_ref[...]
    x1 = x_ref[...] + gt1 * _dot(merged.astype(BF16), wo_ref[...])
    xn2 = (_rms(x1) * g2n_ref[...] * (1.0 + sc2) + sh2).astype(BF16)
    hid = jnp.maximum(_dot(xn2, wff1_ref[...]), 0.0)
    x2 = x1 + gt2 * _dot((hid * hid).astype(BF16), wff2_ref[...])
    o_ref[...] = _rms(x2) * gfin_ref[...]


def _post(y, gout, bonus, ga, gbyb, x, mod, tokens_per_mod, wts):
    ntok, d = x.shape
    tm = TOKEN_TILE
    npair = d // LANES
    flat_spec = pl.BlockSpec((tm, d), lambda i: (i, 0))
    names = ("lnx_w", "lnx_b", "w_pa", "w_o", "g2n", "w_ff1", "w_ff2", "g_fin")
    return pl.pallas_call(
        functools.partial(_post_kernel, d),
        grid=(ntok // tm,),
        in_specs=[pl.BlockSpec((2, npair, tm, LANES), lambda i: (0, 0, i, 0)),
                  flat_spec, flat_spec, flat_spec, flat_spec, flat_spec,
                  pl.BlockSpec((1,) + mod.shape[1:], lambda i: ((i * tm) // tokens_per_mod, 0, 0))]
                 + [_resident(wts[n].shape) for n in names],
        out_specs=flat_spec,
        out_shape=jax.ShapeDtypeStruct((ntok, d), F32),
        scratch_shapes=[pltpu.VMEM((tm, d), BF16)],
        compiler_params=pltpu.CompilerParams(dimension_semantics=("parallel",), vmem_limit_bytes=VMEM_LIMIT),
        name="post_mlp",
    )(y, gout, bonus, ga, gbyb, x, mod, *[wts[n] for n in names])


def _state_to_pairs(s):
    b, h, n, _ = s.shape
    st = jnp.swapaxes(s, -1, -2).reshape(b, h // PAIR, PAIR, n, n)
    bd = jnp.einsum("bphkv,hg->bphkgv", st, jnp.eye(PAIR, dtype=s.dtype))
    return bd.reshape(b, h // PAIR, PAIR * n, PAIR * n)


def _pairs_to_state(bd):
    b, npair, n2, _ = bd.shape
    n = n2 // PAIR
    x = bd.reshape(b, npair, PAIR, n, PAIR, n)
    st = jnp.stack([x[:, :, h, :, h, :] for h in range(PAIR)], axis=2)
    return jnp.swapaxes(st, -1, -2).reshape(b, npair * PAIR, n, n)


def _layer(x3, mod, s0_f, s0_b, seg, wts):
    batch, t, d = x3.shape
    x = x3.reshape(batch * t, d)
    tokens_per_mod = t if mod.shape[0] == batch else batch * t
    r, k, v, a, lw, gout, bonus, ga, gbyb = _in_proj(x, mod, tokens_per_mod, seg, wts)
    s0 = jnp.stack([_state_to_pairs(s0_f), _state_to_pairs(s0_b)], axis=0)
    y, sfin = _scan(r, k, v, a, lw, wts["k_k_pm"], wts["k_a_pm"], s0, batch)
    out = _post(y, gout, bonus, ga, gbyb, x, mod, tokens_per_mod, wts)
    return out.reshape(batch, t, d), _pairs_to_state(sfin[0]), _pairs_to_state(sfin[1])


def kernel(x_prompt, x_sample, state_rwkv_fwd, state_rwkv_bwd, c, c_ctx, norm1_g, norm2_g, w_ada, b_ada, w_in, w0, w2, a0, a2, g2, k_k, k_a, r_k, lnx_w, lnx_b, conv_w, w_pa, w_pb, w_o, w_ff1, w_ff2, final_norm_g):
    depth, d = norm1_g.shape
    assert depth == 1, "the fused final norm assumes a single layer"
    n_heads = r_k.shape[1]
    assert n_heads * HEAD_DIM == d and r_k.shape[2] == HEAD_DIM
    npair = d // LANES
    lora = w2.shape[2]
    assert 2 * lora == LANES and g2.shape[1] == LANES and a2.shape[2] == lora

    def blockdiag2(w):
        z = jnp.zeros_like(w[0])
        return jnp.concatenate([jnp.concatenate([w[0], z], axis=1), jnp.concatenate([z, w[1]], axis=1)], axis=0)

    wts = {
        "g1": norm1_g, "w_in": w_in[0].astype(BF16),
        "w2cat": blockdiag2(w2[0]).astype(BF16), "w0cat": w0[0].reshape(1, 2 * d),
        "a2cat": blockdiag2(a2[0]).astype(BF16), "a0cat": a0[0].reshape(1, 2 * d),
        "g2": g2[0].astype(BF16), "conv_w": conv_w[0], "w_pb": w_pb[0].astype(BF16),
        "k_a": k_a, "r_k": r_k[0].reshape(1, d),
        "k_k_pm": k_k[0].reshape(npair, 1, LANES), "k_a_pm": k_a[0].reshape(npair, 1, LANES),
        "lnx_w": lnx_w, "lnx_b": lnx_b, "w_pa": w_pa[0].astype(BF16), "w_o": w_o[0].astype(BF16),
        "g2n": norm2_g, "w_ff1": w_ff1[0].astype(BF16), "w_ff2": w_ff2[0].astype(BF16),
        "g_fin": final_norm_g.reshape(1, d),
    }

    n_lat = c.shape[0]
    rows = -(-(1 + n_lat) // 8) * 8
    cv = jnp.concatenate([c_ctx[None, :], c, jnp.zeros((rows - 1 - n_lat, d), F32)], axis=0)
    mod = _ada(cv, w_ada[0], b_ada).reshape(rows, 6, d)

    bp = x_prompt.shape[0]
    zero_state = jnp.zeros((bp, n_heads, HEAD_DIM, HEAD_DIM), F32)
    y_prompt, sf, sb = _layer(x_prompt, mod[0:1], zero_state, zero_state, x_prompt.shape[1], wts)
    y_sample, _, _ = _layer(x_sample, mod[1:1 + n_lat], state_rwkv_fwd[:, 0], state_rwkv_bwd[:, 0], GRID_W, wts)
    return (y_prompt, y_sample, sf[:, None], sb[:, None])
```

```python
import functools

import jax
import jax.numpy as jnp
from jax import lax
from jax.experimental import pallas as pl
from jax.experimental.pallas import tpu as pltpu

F32 = jnp.float32
BF16 = jnp.bfloat16

HEAD_DIM = 64
LANES = 128
PAIR = LANES // HEAD_DIM
GRID_W = 64
EPS = 1e-6
LNX_EPS = 64e-5
CHUNK = 64
TOKEN_TILE = 256
PAIR_GROUP = 8
VMEM_LIMIT = 58 * 1024 * 1024


def _resident(shape):
    nd = len(shape)
    return pl.BlockSpec(shape, lambda *_: (0,) * nd, pipeline_mode=pl.Buffered(1))


def _dot(a, b):
    return jnp.dot(a, b, preferred_element_type=F32)


def _softplus(x):
    return jnp.maximum(x, 0.0) + jnp.log1p(jnp.exp(-jnp.abs(x)))


def _head_sum(x, m0):
    lo = jnp.sum(jnp.where(m0, x, 0.0), axis=-1, keepdims=True)
    hi = jnp.sum(jnp.where(m0, 0.0, x), axis=-1, keepdims=True)
    return jnp.where(m0, lo, hi)


def _rms(x):
    return x * lax.rsqrt(jnp.mean(x * x, axis=-1, keepdims=True) + EPS)


def _ada_kernel(c_ref, w_ref, b_ref, o_ref):
    c = c_ref[...]
    s = (c * jax.nn.sigmoid(c)).astype(BF16)
    o_ref[...] = _dot(s, w_ref[...].astype(BF16)) + b_ref[...]


def _ada(cv, w_ada, b_ada):
    rows, d = cv.shape
    n = w_ada.shape[1]
    tn = d
    return pl.pallas_call(
        _ada_kernel,
        grid=(n // tn,),
        in_specs=[
            pl.BlockSpec((rows, d), lambda j: (0, 0)),
            pl.BlockSpec((d, tn), lambda j: (0, j)),
            pl.BlockSpec((1, tn), lambda j: (0, j)),
        ],
        out_specs=pl.BlockSpec((rows, tn), lambda j: (0, j)),
        out_shape=jax.ShapeDtypeStruct((rows, n), F32),
        name="ada_mod",
    )(cv, w_ada, b_ada)


def _in_proj_kernel(seg, d, x_ref, mod_ref, g1_ref, win_ref, w2_ref, w0_ref, a2_ref, a0_ref, g2_ref,
                    cw_ref, wpb_ref, ka_ref, rk_ref,
                    r_ref, k_ref, v_ref, a_ref, lw_ref, gout_ref, bonus_ref, ga_ref, gbyb_ref):
    tm = x_ref.shape[0]
    npair = d // LANES
    mod = mod_ref[0]
    sh1 = mod[0:1]
    sc1 = mod[1:2]
    xn = _rms(x_ref[...]) * g1_ref[...]
    xn = (xn * (1.0 + sc1) + sh1).astype(BF16)

    def proj(lo, hi):
        return _dot(xn, win_ref[:, lo:hi])

    o_lora = 3 * d
    zl = proj(o_lora, o_lora + 3 * LANES)
    wl = zl[:, 0:LANES]
    al = zl[:, LANES:2 * LANES]
    gl = zl[:, 2 * LANES:3 * LANES]
    pre_w = _dot(jnp.tanh(wl).astype(BF16), w2_ref[...]) + w0_ref[...]
    lw = -jnp.exp(-_softplus(-pre_w) - 0.5)
    a = jax.nn.sigmoid(_dot(al.astype(BF16), a2_ref[...]) + a0_ref[...])
    gout_ref[...] = _dot(jax.nn.sigmoid(gl).astype(BF16), g2_ref[...])

    r = proj(0, d)
    k = proj(d, 2 * d)
    v = proj(2 * d, 3 * d)
    q = r * rk_ref[...] * k * (2.0 + (a[:, :d] + a[:, d:] - 2.0) * ka_ref[...])
    m0 = lax.broadcasted_iota(jnp.int32, (1, LANES), 1) < HEAD_DIM
    for p in range(npair):
        sl = slice(p * LANES, (p + 1) * LANES)
        r_ref[p] = r[:, sl]
        k_ref[p] = k[:, sl]
        v_ref[p] = v[:, sl]
        for dr in range(2):
            sd = slice(dr * d + p * LANES, dr * d + (p + 1) * LANES)
            a_ref[dr, p] = a[:, sd]
            lw_ref[dr, p] = lw[:, sd]
        bonus_ref[:, sl] = v[:, sl] * _head_sum(q[:, sl], m0)

    o_conv = o_lora + 3 * LANES
    cb = proj(o_conv, o_conv + d)
    u = proj(o_conv + d, o_conv + 2 * d) * proj(o_conv + 2 * d, o_conv + 3 * d)
    pos = lax.broadcasted_iota(jnp.int32, (tm, 1), 0) % seg
    u_prev = jnp.where(pos == 0, 0.0, pltpu.roll(u, 1, 0))
    u_next = jnp.where(pos == seg - 1, 0.0, pltpu.roll(u, tm - 1, 0))
    conv = cw_ref[0:1] * u_prev + cw_ref[1:2] * u + cw_ref[2:3] * u_next
    yb = _dot((cb * conv).astype(BF16), wpb_ref[...])

    o_gate = o_conv + 3 * d
    ga_ref[...] = jax.nn.sigmoid(proj(o_gate, o_gate + d))
    gbyb_ref[...] = jax.nn.sigmoid(proj(o_gate + d, o_gate + 2 * d)) * yb


def _in_proj(x, mod, tokens_per_mod, seg, wts):
    ntok, d = x.shape
    tm = TOKEN_TILE
    npair = d // LANES
    flat = jax.ShapeDtypeStruct((ntok, d), F32)
    pm = jax.ShapeDtypeStruct((npair, ntok, LANES), F32)
    pm2 = jax.ShapeDtypeStruct((2, npair, ntok, LANES), F32)
    flat_spec = pl.BlockSpec((tm, d), lambda i: (i, 0))
    pm_spec = pl.BlockSpec((npair, tm, LANES), lambda i: (0, i, 0))
    pm2_spec = pl.BlockSpec((2, npair, tm, LANES), lambda i: (0, 0, i, 0))
    names = ("g1", "w_in", "w2cat", "w0cat", "a2cat", "a0cat", "g2", "conv_w", "w_pb", "k_a", "r_k")
    return pl.pallas_call(
        functools.partial(_in_proj_kernel, seg, d),
        grid=(ntok // tm,),
        in_specs=[flat_spec,
                  pl.BlockSpec((1,) + mod.shape[1:], lambda i: ((i * tm) // tokens_per_mod, 0, 0))]
                 + [_resident(wts[n].shape) for n in names],
        out_specs=[pm_spec, pm_spec, pm_spec, pm2_spec, pm2_spec, flat_spec, flat_spec, flat_spec, flat_spec],
        out_shape=[pm, pm, pm, pm2, pm2, flat, flat, flat, flat],
        compiler_params=pltpu.CompilerParams(dimension_semantics=("parallel",), vmem_limit_bytes=VMEM_LIMIT),
        name="in_proj",
    )(x, mod, *[wts[n] for n in names])


def _scan_kernel(n_chunks, r_ref, k_ref, v_ref, a_ref, lw_ref, kkw_ref, ka_ref, s0_ref,
                 y_ref, sfin_ref, h_ref):
    dr = pl.program_id(1)
    c = pl.program_id(2)
    npair = r_ref.shape[0]
    cl = r_ref.shape[1]
    n2 = PAIR * cl

    @pl.when(c == 0)
    def _():
        h_ref[...] = s0_ref[0, 0]

    sgn = 1 - 2 * dr
    row = lax.broadcasted_iota(jnp.int32, (n2, n2), 0)
    col = lax.broadcasted_iota(jnp.int32, (n2, n2), 1)
    same = (row // cl) == (col // cl)
    dt = ((row % cl) - (col % cl)) * sgn
    mask_s = same & (dt > 0)
    mask_i = same & (dt >= 0)
    eye = (row == col).astype(F32)
    tr = lax.broadcasted_iota(jnp.int32, (cl, cl), 0)
    tc = lax.broadcasted_iota(jnp.int32, (cl, cl), 1)
    tri = (((tr - tc) * sgn) >= 0).astype(BF16)
    m0 = lax.broadcasted_iota(jnp.int32, (1, LANES), 1) < HEAD_DIM

    def stack(x):
        return jnp.concatenate([jnp.where(m0, x, 0.0), jnp.where(m0, 0.0, x)], axis=0)

    def each(f, *lists):
        return [f(*xs) for xs in zip(*lists)]

    def group_body(ps):
        r = [r_ref[p] for p in ps]
        k = [k_ref[p] for p in ps]
        v = [v_ref[p] for p in ps]
        a = [a_ref[0, p] for p in ps]
        lw = [lw_ref[0, p] for p in ps]
        kkr = [x * kkw_ref[p] for x, p in zip(k, ps)]
        kk = each(lambda x: x * lax.rsqrt(jnp.maximum(_head_sum(x * x, m0), 1e-12)), kkr)
        kd = [x * (1.0 + (y - 1.0) * ka_ref[p]) for x, y, p in zip(k, a, ps)]
        bv = each(lambda x, y: x * y, kk, a)

        def cumsum(x):
            hi = x.astype(BF16)
            rem = x - hi.astype(F32)
            mid = rem.astype(BF16)
            lo = (rem - mid.astype(F32)).astype(BF16)
            cs3 = _dot(tri, jnp.concatenate([hi, mid, lo], axis=1))
            return cs3[:, :LANES] + cs3[:, LANES:2 * LANES] + cs3[:, 2 * LANES:]

        cs = each(cumsum, lw)
        tot = each(lambda x: jnp.sum(x, axis=0, keepdims=True), lw)
        e_neg = each(lambda x: jnp.exp(-x), cs)
        e_rem = each(lambda t, x: jnp.exp(t - x), tot, cs)
        rt_st = each(lambda x, y: stack(x * jnp.exp(y)), r, cs)
        at_st = each(lambda x, y, z: stack(-x * jnp.exp(y - z)).astype(BF16), kk, cs, lw)
        ktb = each(lambda x, y: (x * y).astype(BF16), kd, e_neg)
        btb = each(lambda x, y: (x * y).astype(BF16), bv, e_neg)
        khs = each(lambda x, y, z: jnp.concatenate([stack(x * z), stack(y * z)], axis=0).astype(BF16),
                   kd, bv, e_rem)
        v_st = each(lambda x: stack(x).astype(BF16), v)

        def scores(at, rt, bt, kt):
            lhs = jnp.concatenate([at, rt.astype(BF16)], axis=0)
            rhs = jnp.concatenate([bt, bt, kt, kt], axis=0)
            return lax.dot_general(lhs, rhs, (((1,), (1,)), ((), ())), preferred_element_type=F32)

        o1 = each(scores, at_st, rt_st, btb, ktb)
        a_ab = each(lambda o: jnp.where(mask_s, o[:n2, :n2], 0.0), o1)
        a_ak = each(lambda o: jnp.where(mask_s, o[:n2, n2:], 0.0).astype(BF16), o1)
        a_r = each(lambda o: jnp.concatenate([jnp.where(mask_i, o[n2:, n2:], 0.0),
                                               jnp.where(mask_i, o[n2:, :n2], 0.0)], axis=1).astype(BF16), o1)

        t_inv = each(lambda x: eye + x, a_ab)
        npow = each(lambda x: x.astype(BF16), a_ab)
        for _ in range(cl.bit_length() - 2):
            npow = each(lambda x: _dot(x, x).astype(BF16), npow)
            t_inv = each(lambda t, n: t + _dot(n, t.astype(BF16)), t_inv, npow)

        x1 = each(lambda x, y: _dot(x, y).astype(BF16), a_ak, v_st)
        ut = each(lambda t, x, y: _dot(t.astype(BF16), jnp.concatenate([x, y], axis=1)).astype(BF16),
                  t_inv, x1, at_st)
        rhs2 = each(lambda x, u: jnp.concatenate(
            [jnp.concatenate([x, jnp.zeros_like(x)], axis=1), u], axis=0), v_st, ut)
        o2a = each(_dot, a_r, rhs2)
        o2b = each(lambda x, y: lax.dot_general(x, y, (((0,), (0,)), ((), ())), preferred_element_type=F32),
                   khs, rhs2)

        h = [h_ref[p] for p in ps]
        hb = each(lambda x: x.astype(BF16), h)
        y_st = each(lambda o, rt, x: o[:, :n2] + _dot((rt + o[:, n2:]).astype(BF16), x), o2a, rt_st, hb)
        gam_col = each(lambda t: jnp.transpose(jnp.broadcast_to(jnp.exp(t), (LANES, LANES))), tot)
        h_new = each(lambda gc, x, o, xb: gc * x + _dot(o[:, n2:].astype(BF16), xb) + o[:, :n2],
                     gam_col, h, o2b, hb)
        for p, y, hn in zip(ps, y_st, h_new):
            y_ref[0, p] = y[:cl] + y[cl:]
            h_ref[p] = hn

    for g0 in range(0, npair, PAIR_GROUP):
        group_body(list(range(g0, min(g0 + PAIR_GROUP, npair))))

    @pl.when(c == n_chunks - 1)
    def _():
        sfin_ref[0, 0] = h_ref[...]


def _scan(r, k, v, a, lw, kkw, ka, s0, batch):
    npair, ntok, _ = r.shape
    cl = CHUNK
    n_chunks = ntok // batch // cl

    def tok_idx(b, dr, c):
        return b * n_chunks + c + dr * (n_chunks - 1 - 2 * c)

    shared = pl.BlockSpec((npair, cl, LANES), lambda b, dr, c: (0, tok_idx(b, dr, c), 0))
    per_dir = pl.BlockSpec((1, npair, cl, LANES), lambda b, dr, c: (dr, 0, tok_idx(b, dr, c), 0))
    state = pl.BlockSpec((1, 1, npair, LANES, LANES), lambda b, dr, c: (dr, b, 0, 0, 0))
    return pl.pallas_call(
        functools.partial(_scan_kernel, n_chunks),
        grid=(batch, 2, n_chunks),
        in_specs=[shared, shared, shared, per_dir, per_dir,
                  _resident(kkw.shape), _resident(ka.shape), state],
        out_specs=[per_dir, state],
        out_shape=[jax.ShapeDtypeStruct((2, npair, ntok, LANES), F32),
                   jax.ShapeDtypeStruct(s0.shape, F32)],
        scratch_shapes=[pltpu.VMEM((npair, LANES, LANES), F32)],
        compiler_params=pltpu.CompilerParams(
            dimension_semantics=("parallel", "arbitrary", "arbitrary"), vmem_limit_bytes=VMEM_LIMIT),
        name="wkv_scan",
    )(r, k, v, a, lw, kkw, ka, s0)


def _post_kernel(d, y_ref, gout_ref, bonus_ref, ga_ref, gbyb_ref, x_ref, mod_ref, lnw_ref, lnb_ref,
                 wpa_ref, wo_ref, g2n_ref, wff1_ref, wff2_ref, gfin_ref, o_ref, xa_ref):
    npair = d // LANES
    mod = mod_ref[0]
    gt1 = mod[2:3]
    sh2 = mod[3:4]
    sc2 = mod[4:5]
    gt2 = mod[5:6]
    m0 = lax.broadcasted_iota(jnp.int32, (1, LANES), 1) < HEAD_DIM
    inv_n = 1.0 / HEAD_DIM
    for p in range(npair):
        sl = slice(p * LANES, (p + 1) * LANES)
        wkv = y_ref[0, p] + y_ref[1, p]
        cen = wkv - _head_sum(wkv, m0) * inv_n
        var = _head_sum(cen * cen, m0) * inv_n
        ln = cen * lax.rsqrt(var + LNX_EPS) * lnw_ref[:, sl] + lnb_ref[:, sl]
        xa_ref[:, sl] = ((ln + bonus_ref[:, sl]) * gout_ref[:, sl]).astype(BF16)
    ya = _dot(xa_ref[...], wpa_ref[...])
    merged = ga_ref[...] * ya + gbyb_ref[...]
    x1 = x_ref[...] + gt1 * _dot(merged.astype(BF16), wo_ref[...])
    xn2 = (_rms(x1) * g2n_ref[...] * (1.0 + sc2) + sh2).astype(BF16)
    hid = jnp.maximum(_dot(xn2, wff1_ref[...]), 0.0)
    x2 = x1 + gt2 * _dot((hid * hid).astype(BF16), wff2_ref[...])
    o_ref[...] = _rms(x2) * gfin_ref[...]


def _post(y, gout, bonus, ga, gbyb, x, mod, tokens_per_mod, wts):
    ntok, d = x.shape
    tm = TOKEN_TILE
    npair = d // LANES
    flat_spec = pl.BlockSpec((tm, d), lambda i: (i, 0))
    names = ("lnx_w", "lnx_b", "w_pa", "w_o", "g2n", "w_ff1", "w_ff2", "g_fin")
    return pl.pallas_call(
        functools.partial(_post_kernel, d),
        grid=(ntok // tm,),
        in_specs=[pl.BlockSpec((2, npair, tm, LANES), lambda i: (0, 0, i, 0)),
                  flat_spec, flat_spec, flat_spec, flat_spec, flat_spec,
                  pl.BlockSpec((1,) + mod.shape[1:], lambda i: ((i * tm) // tokens_per_mod, 0, 0))]
                 + [_resident(wts[n].shape) for n in names],
        out_specs=flat_spec,
        out_shape=jax.ShapeDtypeStruct((ntok, d), F32),
        scratch_shapes=[pltpu.VMEM((tm, d), BF16)],
        compiler_params=pltpu.CompilerParams(dimension_semantics=("parallel",), vmem_limit_bytes=VMEM_LIMIT),
        name="post_mlp",
    )(y, gout, bonus, ga, gbyb, x, mod, *[wts[n] for n in names])


def _state_to_pairs(s):
    b, h, n, _ = s.shape
    st = jnp.swapaxes(s, -1, -2).reshape(b, h // PAIR, PAIR, n, n)
    bd = jnp.einsum("bphkv,hg->bphkgv", st, jnp.eye(PAIR, dtype=s.dtype))
    return bd.reshape(b, h // PAIR, PAIR * n, PAIR * n)


def _pairs_to_state(bd):
    b, npair, n2, _ = bd.shape
    n = n2 // PAIR
    x = bd.reshape(b, npair, PAIR, n, PAIR, n)
    st = jnp.stack([x[:, :, h, :, h, :] for h in range(PAIR)], axis=2)
    return jnp.swapaxes(st, -1, -2).reshape(b, npair * PAIR, n, n)


def _layer(x3, mod, s0_f, s0_b, seg, wts):
    batch, t, d = x3.shape
    x = x3.reshape(batch * t, d)
    tokens_per_mod = t if mod.shape[0] == batch else batch * t
    r, k, v, a, lw, gout, bonus, ga, gbyb = _in_proj(x, mod, tokens_per_mod, seg, wts)
    s0 = jnp.stack([_state_to_pairs(s0_f), _state_to_pairs(s0_b)], axis=0)
    y, sfin = _scan(r, k, v, a, lw, wts["k_k_pm"], wts["k_a_pm"], s0, batch)
    out = _post(y, gout, bonus, ga, gbyb, x, mod, tokens_per_mod, wts)
    return out.reshape(batch, t, d), _pairs_to_state(sfin[0]), _pairs_to_state(sfin[1])


def kernel(x_prompt, x_sample, state_rwkv_fwd, state_rwkv_bwd, c, c_ctx, norm1_g, norm2_g, w_ada, b_ada, w_in, w0, w2, a0, a2, g2, k_k, k_a, r_k, lnx_w, lnx_b, conv_w, w_pa, w_pb, w_o, w_ff1, w_ff2, final_norm_g):
    depth, d = norm1_g.shape
    assert depth == 1, "the fused final norm assumes a single layer"
    n_heads = r_k.shape[1]
    assert n_heads * HEAD_DIM == d and r_k.shape[2] == HEAD_DIM
    npair = d // LANES
    lora = w2.shape[2]
    assert 2 * lora == LANES and g2.shape[1] == LANES and a2.shape[2] == lora

    def blockdiag2(w):
        z = jnp.zeros_like(w[0])
        return jnp.concatenate([jnp.concatenate([w[0], z], axis=1), jnp.concatenate([z, w[1]], axis=1)], axis=0)

    wts = {
        "g1": norm1_g, "w_in": w_in[0].astype(BF16),
        "w2cat": blockdiag2(w2[0]).astype(BF16), "w0cat": w0[0].reshape(1, 2 * d),
        "a2cat": blockdiag2(a2[0]).astype(BF16), "a0cat": a0[0].reshape(1, 2 * d),
        "g2": g2[0].astype(BF16), "conv_w": conv_w[0], "w_pb": w_pb[0].astype(BF16),
        "k_a": k_a, "r_k": r_k[0].reshape(1, d),
        "k_k_pm": k_k[0].reshape(npair, 1, LANES), "k_a_pm": k_a[0].reshape(npair, 1, LANES),
        "lnx_w": lnx_w, "lnx_b": lnx_b, "w_pa": w_pa[0].astype(BF16), "w_o": w_o[0].astype(BF16),
        "g2n": norm2_g, "w_ff1": w_ff1[0].astype(BF16), "w_ff2": w_ff2[0].astype(BF16),
        "g_fin": final_norm_g.reshape(1, d),
    }

    n_lat = c.shape[0]
    rows = -(-(1 + n_lat) // 8) * 8
    cv = jnp.concatenate([c_ctx[None, :], c, jnp.zeros((rows - 1 - n_lat, d), F32)], axis=0)
    mod = _ada(cv, w_ada[0], b_ada).reshape(rows, 6, d)

    bp = x_prompt.shape[0]
    zero_state = jnp.zeros((bp, n_heads, HEAD_DIM, HEAD_DIM), F32)
    y_prompt, sf, sb = _layer(x_prompt, mod[0:1], zero_state, zero_state, x_prompt.shape[1], wts)
    y_sample, _, _ = _layer(x_sample, mod[1:1 + n_lat], state_rwkv_fwd[:, 0], state_rwkv_bwd[:, 0], GRID_W, wts)
    return (y_prompt, y_sample, sf[:, None], sb[:, None])
```

```python
import functools

import jax
import jax.numpy as jnp
from jax import lax
from jax.experimental import pallas as pl
from jax.experimental.pallas import tpu as pltpu

F32 = jnp.float32
BF16 = jnp.bfloat16

HEAD_DIM = 64
LANES = 128
PAIR = LANES // HEAD_DIM
GRID_W = 64
EPS = 1e-6
LNX_EPS = 64e-5
CHUNK = 64
TOKEN_TILE = 256
CHAIN_GROUP = 16
VMEM_LIMIT = 58 * 1024 * 1024


def _resident(shape):
    nd = len(shape)
    return pl.BlockSpec(shape, lambda *_: (0,) * nd, pipeline_mode=pl.Buffered(1))


def _dot(a, b):
    return jnp.dot(a, b, preferred_element_type=F32)


def _softplus(x):
    return jnp.maximum(x, 0.0) + jnp.log1p(jnp.exp(-jnp.abs(x)))


def _head_sum(x, m0):
    lo = jnp.sum(jnp.where(m0, x, 0.0), axis=-1, keepdims=True)
    hi = jnp.sum(jnp.where(m0, 0.0, x), axis=-1, keepdims=True)
    return jnp.where(m0, lo, hi)


def _rms(x):
    return x * lax.rsqrt(jnp.mean(x * x, axis=-1, keepdims=True) + EPS)


def _ada_kernel(c_ref, w_ref, b_ref, o_ref):
    c = c_ref[...]
    s = (c * jax.nn.sigmoid(c)).astype(BF16)
    o_ref[...] = _dot(s, w_ref[...].astype(BF16)) + b_ref[...]


def _ada(cv, w_ada, b_ada):
    rows, d = cv.shape
    n = w_ada.shape[1]
    tn = d
    return pl.pallas_call(
        _ada_kernel,
        grid=(n // tn,),
        in_specs=[
            pl.BlockSpec((rows, d), lambda j: (0, 0)),
            pl.BlockSpec((d, tn), lambda j: (0, j)),
            pl.BlockSpec((1, tn), lambda j: (0, j)),
        ],
        out_specs=pl.BlockSpec((rows, tn), lambda j: (0, j)),
        out_shape=jax.ShapeDtypeStruct((rows, n), F32),
        name="ada_mod",
    )(cv, w_ada, b_ada)


def _in_proj_kernel(seg, d, x_ref, mod_ref, g1_ref, win_ref, w2_ref, w0_ref, a2_ref, a0_ref, g2_ref,
                    cw_ref, wpb_ref, ka_ref, rk_ref,
                    r_ref, k_ref, v_ref, a_ref, lw_ref, gout_ref, bonus_ref, ga_ref, gbyb_ref):
    tm = x_ref.shape[0]
    npair = d // LANES
    mod = mod_ref[0]
    sh1 = mod[0:1]
    sc1 = mod[1:2]
    xn = _rms(x_ref[...]) * g1_ref[...]
    xn = (xn * (1.0 + sc1) + sh1).astype(BF16)

    def proj(lo, hi):
        return _dot(xn, win_ref[:, lo:hi])

    o_lora = 3 * d
    zl = proj(o_lora, o_lora + 3 * LANES)
    wl = zl[:, 0:LANES]
    al = zl[:, LANES:2 * LANES]
    gl = zl[:, 2 * LANES:3 * LANES]
    pre_w = _dot(jnp.tanh(wl).astype(BF16), w2_ref[...]) + w0_ref[...]
    lw = -jnp.exp(-_softplus(-pre_w) - 0.5)
    a = jax.nn.sigmoid(_dot(al.astype(BF16), a2_ref[...]) + a0_ref[...])
    gout_ref[...] = _dot(jax.nn.sigmoid(gl).astype(BF16), g2_ref[...])

    r = proj(0, d)
    k = proj(d, 2 * d)
    v = proj(2 * d, 3 * d)
    q = r * rk_ref[...] * k * (2.0 + (a[:, :d] + a[:, d:] - 2.0) * ka_ref[...])
    m0 = lax.broadcasted_iota(jnp.int32, (1, LANES), 1) < HEAD_DIM
    for p in range(npair):
        sl = slice(p * LANES, (p + 1) * LANES)
        r_ref[p] = r[:, sl]
        k_ref[p] = k[:, sl]
        v_ref[p] = v[:, sl]
        for dr in range(2):
            sd = slice(dr * d + p * LANES, dr * d + (p + 1) * LANES)
            a_ref[dr, p] = a[:, sd]
            lw_ref[dr, p] = lw[:, sd]
        bonus_ref[:, sl] = v[:, sl] * _head_sum(q[:, sl], m0)

    o_conv = o_lora + 3 * LANES
    cb = proj(o_conv, o_conv + d)
    u = proj(o_conv + d, o_conv + 2 * d) * proj(o_conv + 2 * d, o_conv + 3 * d)
    pos = lax.broadcasted_iota(jnp.int32, (tm, 1), 0) % seg
    u_prev = jnp.where(pos == 0, 0.0, pltpu.roll(u, 1, 0))
    u_next = jnp.where(pos == seg - 1, 0.0, pltpu.roll(u, tm - 1, 0))
    conv = cw_ref[0:1] * u_prev + cw_ref[1:2] * u + cw_ref[2:3] * u_next
    yb = _dot((cb * conv).astype(BF16), wpb_ref[...])

    o_gate = o_conv + 3 * d
    ga_ref[...] = jax.nn.sigmoid(proj(o_gate, o_gate + d))
    gbyb_ref[...] = jax.nn.sigmoid(proj(o_gate + d, o_gate + 2 * d)) * yb


def _in_proj(x, mod, tokens_per_mod, seg, wts):
    ntok, d = x.shape
    tm = TOKEN_TILE
    npair = d // LANES
    flat = jax.ShapeDtypeStruct((ntok, d), F32)
    pm = jax.ShapeDtypeStruct((npair, ntok, LANES), F32)
    pm2 = jax.ShapeDtypeStruct((2, npair, ntok, LANES), F32)
    flat_spec = pl.BlockSpec((tm, d), lambda i: (i, 0))
    pm_spec = pl.BlockSpec((npair, tm, LANES), lambda i: (0, i, 0))
    pm2_spec = pl.BlockSpec((2, npair, tm, LANES), lambda i: (0, 0, i, 0))
    names = ("g1", "w_in", "w2cat", "w0cat", "a2cat", "a0cat", "g2", "conv_w", "w_pb", "k_a", "r_k")
    return pl.pallas_call(
        functools.partial(_in_proj_kernel, seg, d),
        grid=(ntok // tm,),
        in_specs=[flat_spec,
                  pl.BlockSpec((1,) + mod.shape[1:], lambda i: ((i * tm) // tokens_per_mod, 0, 0))]
                 + [_resident(wts[n].shape) for n in names],
        out_specs=[pm_spec, pm_spec, pm_spec, pm2_spec, pm2_spec, flat_spec, flat_spec, flat_spec, flat_spec],
        out_shape=[pm, pm, pm, pm2, pm2, flat, flat, flat, flat],
        compiler_params=pltpu.CompilerParams(dimension_semantics=("parallel",), vmem_limit_bytes=VMEM_LIMIT),
        name="in_proj",
    )(x, mod, *[wts[n] for n in names])


def _scan_kernel(n_chunks, rf_ref, kf_ref, vf_ref, rb_ref, kb_ref, vb_ref, af_ref, lwf_ref, ab_ref, lwb_ref,
                 kkw_ref, ka_ref, s0_ref, yf_ref, yb_ref, sfin_ref, h_ref):
    c = pl.program_id(1)
    npair = rf_ref.shape[0]
    cl = rf_ref.shape[1]
    r_refs, k_refs, v_refs = (rf_ref, rb_ref), (kf_ref, kb_ref), (vf_ref, vb_ref)
    a_refs, lw_refs, y_refs = (af_ref, ab_ref), (lwf_ref, lwb_ref), (yf_ref, yb_ref)

    @pl.when(c == 0)
    def _():
        h_ref[...] = s0_ref[:, 0]

    row = lax.broadcasted_iota(jnp.int32, (cl, LANES), 0)
    col = lax.broadcasted_iota(jnp.int32, (cl, LANES), 1) % HEAD_DIM
    dt = row - col
    mask_s = (dt > 0, dt < 0)
    mask_i = (dt >= 0, dt <= 0)
    eye = (row == col).astype(F32)
    tri = tuple(m[:, :cl].astype(BF16) for m in mask_i)
    m0 = lax.broadcasted_iota(jnp.int32, (1, LANES), 1) < HEAD_DIM

    def diag2(x):
        return jnp.concatenate([jnp.where(m0, x, 0.0), jnp.where(m0, 0.0, x)], axis=0)

    def side_by_side_t(x):
        t = jnp.transpose(diag2(x))
        return t[:cl] + t[cl:]

    def each(f, *lists):
        return [f(*xs) for xs in zip(*lists)]

    def group_body(chains):
        r = [r_refs[dr][p] for dr, p in chains]
        k = [k_refs[dr][p] for dr, p in chains]
        v = [v_refs[dr][p] for dr, p in chains]
        a = [a_refs[dr][0, p] for dr, p in chains]
        lw = [lw_refs[dr][0, p] for dr, p in chains]
        ms = [mask_s[dr] for dr, _ in chains]
        mi = [mask_i[dr] for dr, _ in chains]
        kkr = [x * kkw_ref[p] for x, (_, p) in zip(k, chains)]
        kk = each(lambda x: x * lax.rsqrt(jnp.maximum(_head_sum(x * x, m0), 1e-12)), kkr)
        kd = [x * (1.0 + (y - 1.0) * ka_ref[p]) for x, y, (_, p) in zip(k, a, chains)]
        bv = each(lambda x, y: x * y, kk, a)

        def cumsum(x, tri_d):
            hi = x.astype(BF16)
            lo = (x - hi.astype(F32)).astype(BF16)
            cs2 = _dot(tri_d, jnp.concatenate([hi, lo], axis=1))
            return cs2[:, :LANES] + cs2[:, LANES:]

        cs = each(cumsum, lw, [tri[dr] for dr, _ in chains])
        tot = each(lambda x: jnp.sum(x, axis=0, keepdims=True), lw)
        e_neg = each(lambda x: jnp.exp(-x), cs)
        e_rem = each(lambda t, x: jnp.exp(t - x), tot, cs)
        rt = each(lambda x, y: x * jnp.exp(y), r, cs)
        at = each(lambda x, y, z: (-x * jnp.exp(y - z)).astype(BF16), kk, cs, lw)
        kt_d = each(lambda x, y: diag2((x * y).astype(BF16)), kd, e_neg)
        bt_d = each(lambda x, y: diag2((x * y).astype(BF16)), bv, e_neg)
        kb_t = each(lambda x, y, z: jnp.concatenate([side_by_side_t(x * z), side_by_side_t(y * z)], axis=1)
                    .astype(BF16), kd, bv, e_rem)
        v_d = each(lambda x: diag2(x.astype(BF16)), v)

        def scores(at_, rt_, bt_, kt_):
            lhs = jnp.concatenate([at_, rt_.astype(BF16)], axis=0)
            rhs = jnp.concatenate([bt_, kt_], axis=0)
            return lax.dot_general(lhs, rhs, (((1,), (1,)), ((), ())), preferred_element_type=F32)

        o1 = each(scores, at, rt, bt_d, kt_d)
        a_ab = each(lambda o, m: jnp.where(m, o[:cl, :LANES], 0.0), o1, ms)
        a_ak = each(lambda o, m: jnp.where(m, o[:cl, LANES:], 0.0).astype(BF16), o1, ms)
        a_r = each(lambda o, m: jnp.concatenate([jnp.where(m, o[cl:, LANES:], 0.0),
                                                  jnp.where(m, o[cl:, :LANES], 0.0)], axis=1).astype(BF16), o1, mi)

        t_inv = each(lambda x: eye + x, a_ab)
        npow = each(lambda x: x.astype(BF16), a_ab)
        npow = each(lambda x: _dot(x, diag2(x)).astype(BF16), npow)
        for _ in range(cl.bit_length() - 3):
            both = each(lambda n, t: _dot(n, jnp.concatenate([diag2(n), diag2(t.astype(BF16))], axis=1)),
                        npow, t_inv)
            npow = each(lambda x: x[:, :LANES].astype(BF16), both)
            t_inv = each(lambda t, x: t + x[:, LANES:], t_inv, both)
        t_inv = each(lambda t, n: t + _dot(n, diag2(t.astype(BF16))), t_inv, npow)

        x1 = each(lambda x, y: _dot(x, y).astype(BF16), a_ak, v_d)
        ut = each(lambda t, x, y: _dot(t.astype(BF16), jnp.concatenate([diag2(x), diag2(y)], axis=1))
                  .astype(BF16), t_inv, x1, at)
        rhs2 = each(lambda x, u: jnp.concatenate(
            [jnp.concatenate([x, jnp.zeros_like(x)], axis=1),
             jnp.concatenate([diag2(u[:, :LANES]), diag2(u[:, LANES:])], axis=1)], axis=0), v_d, ut)
        o2 = each(lambda x, y, z: _dot(jnp.concatenate([x, y], axis=0), z), a_r, kb_t, rhs2)

        h = [h_ref[dr, p] for dr, p in chains]
        h_d = each(lambda x: diag2(x.astype(BF16)), h)
        y = each(lambda o, rt_, x: o[:cl, :LANES] + _dot((rt_ + o[:cl, LANES:]).astype(BF16), x), o2, rt, h_d)

        def decay_col(t):
            gc = jnp.transpose(jnp.broadcast_to(jnp.exp(t), (LANES, LANES)))
            return jnp.where(m0, gc[:cl], gc[cl:])

        h_new = each(lambda t, x, o, xd: decay_col(t) * x + _dot(o[cl:, LANES:].astype(BF16), xd) + o[cl:, :LANES],
                     tot, h, o2, h_d)
        for (dr, p), y_p, hn in zip(chains, y, h_new):
            y_refs[dr][p] = y_p
            h_ref[dr, p] = hn

    all_chains = [(dr, p) for p in range(npair) for dr in range(2)]
    for g0 in range(0, len(all_chains), CHAIN_GROUP):
        group_body(all_chains[g0:g0 + CHAIN_GROUP])

    @pl.when(c == n_chunks - 1)
    def _():
        sfin_ref[:, 0] = h_ref[...]


def _scan(r, k, v, a, lw, kkw, ka, s0, batch):
    npair, ntok, _ = r.shape
    cl = CHUNK
    assert cl == HEAD_DIM, "the side-by-side tiles assume chunk length == head dim"
    n_chunks = ntok // batch // cl

    def fwd(b, c):
        return b * n_chunks + c

    def bwd(b, c):
        return b * n_chunks + n_chunks - 1 - c

    def shared(idx):
        return pl.BlockSpec((npair, cl, LANES), lambda b, c: (0, idx(b, c), 0))

    def per_dir(dr, idx):
        return pl.BlockSpec((1, npair, cl, LANES), lambda b, c: (dr, 0, idx(b, c), 0))

    state = pl.BlockSpec((2, 1, npair, HEAD_DIM, LANES), lambda b, c: (0, b, 0, 0, 0))
    y_shape = jax.ShapeDtypeStruct((npair, ntok, LANES), F32)
    return pl.pallas_call(
        functools.partial(_scan_kernel, n_chunks),
        grid=(batch, n_chunks),
        in_specs=[shared(fwd), shared(fwd), shared(fwd), shared(bwd), shared(bwd), shared(bwd),
                  per_dir(0, fwd), per_dir(0, fwd), per_dir(1, bwd), per_dir(1, bwd),
                  _resident(kkw.shape), _resident(ka.shape), state],
        out_specs=[shared(fwd), shared(bwd), state],
        out_shape=[y_shape, y_shape, jax.ShapeDtypeStruct(s0.shape, F32)],
        scratch_shapes=[pltpu.VMEM((2, npair, HEAD_DIM, LANES), F32)],
        compiler_params=pltpu.CompilerParams(
            dimension_semantics=("parallel", "arbitrary"), vmem_limit_bytes=VMEM_LIMIT),
        name="wkv_scan",
    )(r, k, v, r, k, v, a, lw, a, lw, kkw, ka, s0)


def _post_kernel(d, yf_ref, yb_ref, gout_ref, bonus_ref, ga_ref, gbyb_ref, x_ref, mod_ref, lnw_ref, lnb_ref,
                 wpa_ref, wo_ref, g2n_ref, wff1_ref, wff2_ref, gfin_ref, o_ref, xa_ref):
    npair = d // LANES
    mod = mod_ref[0]
    gt1 = mod[2:3]
    sh2 = mod[3:4]
    sc2 = mod[4:5]
    gt2 = mod[5:6]
    m0 = lax.broadcasted_iota(jnp.int32, (1, LANES), 1) < HEAD_DIM
    inv_n = 1.0 / HEAD_DIM
    for p in range(npair):
        sl = slice(p * LANES, (p + 1) * LANES)
        wkv = yf_ref[p] + yb_ref[p]
        cen = wkv - _head_sum(wkv, m0) * inv_n
        var = _head_sum(cen * cen, m0) * inv_n
        ln = cen * lax.rsqrt(var + LNX_EPS) * lnw_ref[:, sl] + lnb_ref[:, sl]
        xa_ref[:, sl] = ((ln + bonus_ref[:, sl]) * gout_ref[:, sl]).astype(BF16)
    ya = _dot(xa_ref[...], wpa_ref[...])
    merged = ga_ref[...] * ya + gbyb_ref[...]
    x1 = x_ref[...] + gt1 * _dot(merged.astype(BF16), wo_ref[...])
    xn2 = (_rms(x1) * g2n_ref[...] * (1.0 + sc2) + sh2).astype(BF16)
    hid = jnp.maximum(_dot(xn2, wff1_ref[...]), 0.0)
    x2 = x1 + gt2 * _dot((hid * hid).astype(BF16), wff2_ref[...])
    o_ref[...] = _rms(x2) * gfin_ref[...]


def _post(yf, yb, gout, bonus, ga, gbyb, x, mod, tokens_per_mod, wts):
    ntok, d = x.shape
    tm = TOKEN_TILE
    npair = d // LANES
    flat_spec = pl.BlockSpec((tm, d), lambda i: (i, 0))
    pm_spec = pl.BlockSpec((npair, tm, LANES), lambda i: (0, i, 0))
    names = ("lnx_w", "lnx_b", "w_pa", "w_o", "g2n", "w_ff1", "w_ff2", "g_fin")
    return pl.pallas_call(
        functools.partial(_post_kernel, d),
        grid=(ntok // tm,),
        in_specs=[pm_spec, pm_spec,
                  flat_spec, flat_spec, flat_spec, flat_spec, flat_spec,
                  pl.BlockSpec((1,) + mod.shape[1:], lambda i: ((i * tm) // tokens_per_mod, 0, 0))]
                 + [_resident(wts[n].shape) for n in names],
        out_specs=flat_spec,
        out_shape=jax.ShapeDtypeStruct((ntok, d), F32),
        scratch_shapes=[pltpu.VMEM((tm, d), BF16)],
        compiler_params=pltpu.CompilerParams(dimension_semantics=("parallel",), vmem_limit_bytes=VMEM_LIMIT),
        name="post_mlp",
    )(yf, yb, gout, bonus, ga, gbyb, x, mod, *[wts[n] for n in names])


def _state_to_pairs(s):
    b, h, n, _ = s.shape
    x = s.reshape(b, h // PAIR, PAIR, n, n)
    return jnp.transpose(x, (0, 1, 4, 2, 3)).reshape(b, h // PAIR, n, PAIR * n)


def _pairs_to_state(t):
    b, npair, n, _ = t.shape
    x = t.reshape(b, npair, n, PAIR, n)
    return jnp.transpose(x, (0, 1, 3, 4, 2)).reshape(b, npair * PAIR, n, n)


def _layer(x3, mod, s0_f, s0_b, seg, wts):
    batch, t, d = x3.shape
    x = x3.reshape(batch * t, d)
    tokens_per_mod = t if mod.shape[0] == batch else batch * t
    r, k, v, a, lw, gout, bonus, ga, gbyb = _in_proj(x, mod, tokens_per_mod, seg, wts)
    s0 = jnp.stack([_state_to_pairs(s0_f), _state_to_pairs(s0_b)], axis=0)
    yf, yb, sfin = _scan(r, k, v, a, lw, wts["k_k_pm"], wts["k_a_pm"], s0, batch)
    out = _post(yf, yb, gout, bonus, ga, gbyb, x, mod, tokens_per_mod, wts)
    return out.reshape(batch, t, d), _pairs_to_state(sfin[0]), _pairs_to_state(sfin[1])


def kernel(x_prompt, x_sample, state_rwkv_fwd, state_rwkv_bwd, c, c_ctx, norm1_g, norm2_g, w_ada, b_ada, w_in, w0, w2, a0, a2, g2, k_k, k_a, r_k, lnx_w, lnx_b, conv_w, w_pa, w_pb, w_o, w_ff1, w_ff2, final_norm_g):
    depth, d = norm1_g.shape
    assert depth == 1, "the fused final norm assumes a single layer"
    n_heads = r_k.shape[1]
    assert n_heads * HEAD_DIM == d and r_k.shape[2] == HEAD_DIM
    npair = d // LANES
    lora = w2.shape[2]
    assert 2 * lora == LANES and g2.shape[1] == LANES and a2.shape[2] == lora

    def blockdiag2(w):
        z = jnp.zeros_like(w[0])
        return jnp.concatenate([jnp.concatenate([w[0], z], axis=1), jnp.concatenate([z, w[1]], axis=1)], axis=0)

    wts = {
        "g1": norm1_g, "w_in": w_in[0].astype(BF16),
        "w2cat": blockdiag2(w2[0]).astype(BF16), "w0cat": w0[0].reshape(1, 2 * d),
        "a2cat": blockdiag2(a2[0]).astype(BF16), "a0cat": a0[0].reshape(1, 2 * d),
        "g2": g2[0].astype(BF16), "conv_w": conv_w[0], "w_pb": w_pb[0].astype(BF16),
        "k_a": k_a, "r_k": r_k[0].reshape(1, d),
        "k_k_pm": k_k[0].reshape(npair, 1, LANES), "k_a_pm": k_a[0].reshape(npair, 1, LANES),
        "lnx_w": lnx_w, "lnx_b": lnx_b, "w_pa": w_pa[0].astype(BF16), "w_o": w_o[0].astype(BF16),
        "g2n": norm2_g, "w_ff1": w_ff1[0].astype(BF16), "w_ff2": w_ff2[0].astype(BF16),
        "g_fin": final_norm_g.reshape(1, d),
    }

    n_lat = c.shape[0]
    rows = -(-(1 + n_lat) // 8) * 8
    cv = jnp.concatenate([c_ctx[None, :], c, jnp.zeros((rows - 1 - n_lat, d), F32)], axis=0)
    mod = _ada(cv, w_ada[0], b_ada).reshape(rows, 6, d)

    bp = x_prompt.shape[0]
    zero_state = jnp.zeros((bp, n_heads, HEAD_DIM, HEAD_DIM), F32)
    y_prompt, sf, sb = _layer(x_prompt, mod[0:1], zero_state, zero_state, x_prompt.shape[1], wts)
    y_sample, _, _ = _layer(x_sample, mod[1:1 + n_lat], state_rwkv_fwd[:, 0], state_rwkv_bwd[:, 0], GRID_W, wts)
    return (y_prompt, y_sample, sf[:, None], sb[:, None])
```

```python
import functools

import jax
import jax.numpy as jnp
from jax import lax
from jax.experimental import pallas as pl
from jax.experimental.pallas import tpu as pltpu

F32 = jnp.float32
BF16 = jnp.bfloat16

HEAD_DIM = 64
LANES = 128
PAIR = LANES // HEAD_DIM
GRID_W = 64
EPS = 1e-6
LNX_EPS = 64e-5
CHUNK = 64
TOKEN_TILE = 256
SCAN_SUBCHUNKS = 2
DIR_PAIR_GROUP = 16
VMEM_LIMIT = 58 * 1024 * 1024


def _resident(shape):
    nd = len(shape)
    return pl.BlockSpec(shape, lambda *_: (0,) * nd, pipeline_mode=pl.Buffered(1))


def _dot(a, b):
    return jnp.dot(a, b, preferred_element_type=F32)


def _softplus(x):
    return jnp.maximum(x, 0.0) + jnp.log1p(jnp.exp(-jnp.abs(x)))


def _head_sum(x, m0):
    lo = jnp.sum(jnp.where(m0, x, 0.0), axis=-1, keepdims=True)
    hi = jnp.sum(jnp.where(m0, 0.0, x), axis=-1, keepdims=True)
    return jnp.where(m0, lo, hi)


def _rms(x):
    return x * lax.rsqrt(jnp.mean(x * x, axis=-1, keepdims=True) + EPS)


def _ada_kernel(c_ref, w_ref, b_ref, o_ref):
    c = c_ref[...]
    s = (c * jax.nn.sigmoid(c)).astype(BF16)
    o_ref[...] = _dot(s, w_ref[...].astype(BF16)) + b_ref[...]


def _ada(cv, w_ada, b_ada):
    rows, d = cv.shape
    n = w_ada.shape[1]
    tn = d
    return pl.pallas_call(
        _ada_kernel,
        grid=(n // tn,),
        in_specs=[
            pl.BlockSpec((rows, d), lambda j: (0, 0)),
            pl.BlockSpec((d, tn), lambda j: (0, j)),
            pl.BlockSpec((1, tn), lambda j: (0, j)),
        ],
        out_specs=pl.BlockSpec((rows, tn), lambda j: (0, j)),
        out_shape=jax.ShapeDtypeStruct((rows, n), F32),
        name="ada_mod",
    )(cv, w_ada, b_ada)


def _in_proj_kernel(seg, d, x_ref, mod_ref, g1_ref, win_ref, w2_ref, w0_ref, a2_ref, a0_ref, g2_ref,
                    cw_ref, wpb_ref, ka_ref, rk_ref,
                    r_ref, k_ref, v_ref, a_ref, lw_ref, gout_ref, bonus_ref, ga_ref, gbyb_ref):
    tm = x_ref.shape[0]
    npair = d // LANES
    mod = mod_ref[0]
    sh1 = mod[0:1]
    sc1 = mod[1:2]
    xn = _rms(x_ref[...]) * g1_ref[...]
    xn = (xn * (1.0 + sc1) + sh1).astype(BF16)

    def proj(lo, hi):
        return _dot(xn, win_ref[:, lo:hi])

    o_lora = 3 * d
    zl = proj(o_lora, o_lora + 3 * LANES)
    wl = zl[:, 0:LANES]
    al = zl[:, LANES:2 * LANES]
    gl = zl[:, 2 * LANES:3 * LANES]
    pre_w = _dot(jnp.tanh(wl).astype(BF16), w2_ref[...]) + w0_ref[...]
    lw = -jnp.exp(-_softplus(-pre_w) - 0.5)
    a = jax.nn.sigmoid(_dot(al.astype(BF16), a2_ref[...]) + a0_ref[...])
    gout_ref[...] = _dot(jax.nn.sigmoid(gl).astype(BF16), g2_ref[...])

    r = proj(0, d)
    k = proj(d, 2 * d)
    v = proj(2 * d, 3 * d)
    q = r * rk_ref[...] * k * (2.0 + (a[:, :d] + a[:, d:] - 2.0) * ka_ref[...])
    m0 = lax.broadcasted_iota(jnp.int32, (1, LANES), 1) < HEAD_DIM
    for p in range(npair):
        sl = slice(p * LANES, (p + 1) * LANES)
        r_ref[p] = r[:, sl]
        k_ref[p] = k[:, sl]
        v_ref[p] = v[:, sl]
        for dr in range(2):
            sd = slice(dr * d + p * LANES, dr * d + (p + 1) * LANES)
            a_ref[dr, p] = a[:, sd]
            lw_ref[dr, p] = lw[:, sd]
        bonus_ref[:, sl] = v[:, sl] * _head_sum(q[:, sl], m0)

    o_conv = o_lora + 3 * LANES
    cb = proj(o_conv, o_conv + d)
    u = proj(o_conv + d, o_conv + 2 * d) * proj(o_conv + 2 * d, o_conv + 3 * d)
    pos = lax.broadcasted_iota(jnp.int32, (tm, 1), 0) % seg
    u_prev = jnp.where(pos == 0, 0.0, pltpu.roll(u, 1, 0))
    u_next = jnp.where(pos == seg - 1, 0.0, pltpu.roll(u, tm - 1, 0))
    conv = cw_ref[0:1] * u_prev + cw_ref[1:2] * u + cw_ref[2:3] * u_next
    yb = _dot((cb * conv).astype(BF16), wpb_ref[...])

    o_gate = o_conv + 3 * d
    ga_ref[...] = jax.nn.sigmoid(proj(o_gate, o_gate + d))
    gbyb_ref[...] = jax.nn.sigmoid(proj(o_gate + d, o_gate + 2 * d)) * yb


def _in_proj(x, mod, tokens_per_mod, seg, wts):
    ntok, d = x.shape
    tm = TOKEN_TILE
    npair = d // LANES
    flat = jax.ShapeDtypeStruct((ntok, d), F32)
    pm = jax.ShapeDtypeStruct((npair, ntok, LANES), F32)
    pm2 = jax.ShapeDtypeStruct((2, npair, ntok, LANES), F32)
    flat_spec = pl.BlockSpec((tm, d), lambda i: (i, 0))
    pm_spec = pl.BlockSpec((npair, tm, LANES), lambda i: (0, i, 0))
    pm2_spec = pl.BlockSpec((2, npair, tm, LANES), lambda i: (0, 0, i, 0))
    names = ("g1", "w_in", "w2cat", "w0cat", "a2cat", "a0cat", "g2", "conv_w", "w_pb", "k_a", "r_k")
    return pl.pallas_call(
        functools.partial(_in_proj_kernel, seg, d),
        grid=(ntok // tm,),
        in_specs=[flat_spec,
                  pl.BlockSpec((1,) + mod.shape[1:], lambda i: ((i * tm) // tokens_per_mod, 0, 0))]
                 + [_resident(wts[n].shape) for n in names],
        out_specs=[pm_spec, pm_spec, pm_spec, pm2_spec, pm2_spec, flat_spec, flat_spec, flat_spec, flat_spec],
        out_shape=[pm, pm, pm, pm2, pm2, flat, flat, flat, flat],
        compiler_params=pltpu.CompilerParams(dimension_semantics=("parallel",), vmem_limit_bytes=VMEM_LIMIT),
        name="in_proj",
    )(x, mod, *[wts[n] for n in names])


def _scan_kernel(n_chunks, rf_ref, kf_ref, vf_ref, rb_ref, kb_ref, vb_ref, af_ref, lwf_ref, ab_ref, lwb_ref,
                 kkw_ref, ka_ref, s0_ref, yf_ref, yb_ref, sfin_ref, h_ref):
    c = pl.program_id(1)
    npair = rf_ref.shape[0]
    cl = CHUNK
    nsub = rf_ref.shape[1] // cl
    r_refs, k_refs, v_refs = (rf_ref, rb_ref), (kf_ref, kb_ref), (vf_ref, vb_ref)
    a_refs, lw_refs, y_refs = (af_ref, ab_ref), (lwf_ref, lwb_ref), (yf_ref, yb_ref)

    @pl.when(c == 0)
    def _():
        h_ref[...] = s0_ref[:, 0]

    row = lax.broadcasted_iota(jnp.int32, (cl, LANES), 0)
    col = lax.broadcasted_iota(jnp.int32, (cl, LANES), 1) % HEAD_DIM
    dt = row - col
    mask_s = (dt > 0, dt < 0)
    mask_i = (dt >= 0, dt <= 0)
    eye = (row == col).astype(F32)
    tri = tuple(m[:, :cl].astype(BF16) for m in mask_i)
    m0 = lax.broadcasted_iota(jnp.int32, (1, LANES), 1) < HEAD_DIM

    def diag2(x):
        return jnp.concatenate([jnp.where(m0, x, 0.0), jnp.where(m0, 0.0, x)], axis=0)

    def side_by_side_t(x):
        t = jnp.transpose(diag2(x))
        return t[:cl] + t[cl:]

    def each(f, *lists):
        return [f(*xs) for xs in zip(*lists)]

    def group_body(chains):
        r = [r_refs[dr][p, sub * cl:(sub + 1) * cl] for dr, p, sub in chains]
        k = [k_refs[dr][p, sub * cl:(sub + 1) * cl] for dr, p, sub in chains]
        v = [v_refs[dr][p, sub * cl:(sub + 1) * cl] for dr, p, sub in chains]
        a = [a_refs[dr][0, p, sub * cl:(sub + 1) * cl] for dr, p, sub in chains]
        lw = [lw_refs[dr][0, p, sub * cl:(sub + 1) * cl] for dr, p, sub in chains]
        ms = [mask_s[dr] for dr, _, _ in chains]
        mi = [mask_i[dr] for dr, _, _ in chains]
        kkr = [x * kkw_ref[p] for x, (_, p, _) in zip(k, chains)]
        kk = each(lambda x: x * lax.rsqrt(jnp.maximum(_head_sum(x * x, m0), 1e-12)), kkr)
        kd = [x * (1.0 + (y - 1.0) * ka_ref[p]) for x, y, (_, p, _) in zip(k, a, chains)]
        bv = each(lambda x, y: x * y, kk, a)

        def cumsum(x, tri_d):
            hi = x.astype(BF16)
            lo = (x - hi.astype(F32)).astype(BF16)
            cs2 = _dot(tri_d, jnp.concatenate([hi, lo], axis=1))
            return cs2[:, :LANES] + cs2[:, LANES:]

        cs = each(cumsum, lw, [tri[dr] for dr, _, _ in chains])
        tot = each(lambda x: jnp.sum(x, axis=0, keepdims=True), lw)
        e_neg = each(lambda x: jnp.exp(-x), cs)
        e_rem = each(lambda t, x: jnp.exp(t - x), tot, cs)
        rt = each(lambda x, y: x * jnp.exp(y), r, cs)
        at = each(lambda x, y, z: (-x * jnp.exp(y - z)).astype(BF16), kk, cs, lw)
        kt_d = each(lambda x, y: diag2((x * y).astype(BF16)), kd, e_neg)
        bt_d = each(lambda x, y: diag2((x * y).astype(BF16)), bv, e_neg)
        kb_t = each(lambda x, y, z: jnp.concatenate([side_by_side_t(x * z), side_by_side_t(y * z)], axis=1)
                    .astype(BF16), kd, bv, e_rem)
        v_d = each(lambda x: diag2(x.astype(BF16)), v)

        def scores(at_, rt_, bt_, kt_):
            lhs = jnp.concatenate([at_, rt_.astype(BF16)], axis=0)
            rhs = jnp.concatenate([bt_, kt_], axis=0)
            return lax.dot_general(lhs, rhs, (((1,), (1,)), ((), ())), preferred_element_type=F32)

        o1 = each(scores, at, rt, bt_d, kt_d)
        a_ab = each(lambda o, m: jnp.where(m, o[:cl, :LANES], 0.0), o1, ms)
        a_ak = each(lambda o, m: jnp.where(m, o[:cl, LANES:], 0.0).astype(BF16), o1, ms)
        a_r = each(lambda o, m: jnp.concatenate([jnp.where(m, o[cl:, LANES:], 0.0),
                                                  jnp.where(m, o[cl:, :LANES], 0.0)], axis=1).astype(BF16), o1, mi)

        t_inv = each(lambda x: eye + x, a_ab)
        npow = each(lambda x: x.astype(BF16), a_ab)
        npow = each(lambda x: _dot(x, diag2(x)).astype(BF16), npow)
        for _ in range(cl.bit_length() - 3):
            both = each(lambda n, t: _dot(n, jnp.concatenate([diag2(n), diag2(t.astype(BF16))], axis=1)),
                        npow, t_inv)
            npow = each(lambda x: x[:, :LANES].astype(BF16), both)
            t_inv = each(lambda t, x: t + x[:, LANES:], t_inv, both)
        t_inv = each(lambda t, n: t + _dot(n, diag2(t.astype(BF16))), t_inv, npow)

        x1 = each(lambda x, y: _dot(x, y).astype(BF16), a_ak, v_d)
        ut = each(lambda t, x, y: _dot(t.astype(BF16), jnp.concatenate([diag2(x), diag2(y)], axis=1))
                  .astype(BF16), t_inv, x1, at)
        rhs2 = each(lambda x, u: jnp.concatenate(
            [jnp.concatenate([x, jnp.zeros_like(x)], axis=1),
             jnp.concatenate([diag2(u[:, :LANES]), diag2(u[:, LANES:])], axis=1)], axis=0), v_d, ut)
        o2 = each(lambda x, y, z: _dot(jnp.concatenate([x, y], axis=0), z), a_r, kb_t, rhs2)

        def decay_col(t):
            gc = jnp.transpose(jnp.broadcast_to(jnp.exp(t), (LANES, LANES)))
            return jnp.where(m0, gc[:cl], gc[cl:])

        return dict(zip(chains, zip(o2, rt, each(decay_col, tot))))

    def state_stage(local, dps, step):
        subs = [step if dr == 0 else nsub - 1 - step for dr, _ in dps]
        o2, rt, dec = zip(*[local[(dr, p, sub)] for (dr, p), sub in zip(dps, subs)])
        h = [h_ref[dr, p] for dr, p in dps]
        h_d = each(lambda x: diag2(x.astype(BF16)), h)
        y = each(lambda o, rt_, x: o[:cl, :LANES] + _dot((rt_ + o[:cl, LANES:]).astype(BF16), x), o2, rt, h_d)
        h_new = each(lambda g, x, o, xd: g * x + _dot(o[cl:, LANES:].astype(BF16), xd) + o[cl:, :LANES],
                     dec, h, o2, h_d)
        for (dr, p), sub, y_p, hn in zip(dps, subs, y, h_new):
            y_refs[dr][p, sub * cl:(sub + 1) * cl] = y_p
            h_ref[dr, p] = hn

    dir_pairs = [(dr, p) for p in range(npair) for dr in range(2)]
    for g0 in range(0, len(dir_pairs), DIR_PAIR_GROUP):
        dps = dir_pairs[g0:g0 + DIR_PAIR_GROUP]
        local = group_body([(dr, p, sub) for sub in range(nsub) for dr, p in dps])
        for step in range(nsub):
            state_stage(local, dps, step)

    @pl.when(c == n_chunks - 1)
    def _():
        sfin_ref[:, 0] = h_ref[...]


def _scan(r, k, v, a, lw, kkw, ka, s0, batch):
    npair, ntok, _ = r.shape
    assert CHUNK == HEAD_DIM, "the side-by-side tiles assume chunk length == head dim"
    cl = SCAN_SUBCHUNKS * CHUNK
    n_chunks = ntok // batch // cl

    def fwd(b, c):
        return b * n_chunks + c

    def bwd(b, c):
        return b * n_chunks + n_chunks - 1 - c

    def shared(idx):
        return pl.BlockSpec((npair, cl, LANES), lambda b, c: (0, idx(b, c), 0))

    def per_dir(dr, idx):
        return pl.BlockSpec((1, npair, cl, LANES), lambda b, c: (dr, 0, idx(b, c), 0))

    state = pl.BlockSpec((2, 1, npair, HEAD_DIM, LANES), lambda b, c: (0, b, 0, 0, 0))
    y_shape = jax.ShapeDtypeStruct((npair, ntok, LANES), F32)
    return pl.pallas_call(
        functools.partial(_scan_kernel, n_chunks),
        grid=(batch, n_chunks),
        in_specs=[shared(fwd), shared(fwd), shared(fwd), shared(bwd), shared(bwd), shared(bwd),
                  per_dir(0, fwd), per_dir(0, fwd), per_dir(1, bwd), per_dir(1, bwd),
                  _resident(kkw.shape), _resident(ka.shape), state],
        out_specs=[shared(fwd), shared(bwd), state],
        out_shape=[y_shape, y_shape, jax.ShapeDtypeStruct(s0.shape, F32)],
        scratch_shapes=[pltpu.VMEM((2, npair, HEAD_DIM, LANES), F32)],
        compiler_params=pltpu.CompilerParams(
            dimension_semantics=("parallel", "arbitrary"), vmem_limit_bytes=VMEM_LIMIT),
        name="wkv_scan",
    )(r, k, v, r, k, v, a, lw, a, lw, kkw, ka, s0)


def _post_kernel(d, yf_ref, yb_ref, gout_ref, bonus_ref, ga_ref, gbyb_ref, x_ref, mod_ref, lnw_ref, lnb_ref,
                 wpa_ref, wo_ref, g2n_ref, wff1_ref, wff2_ref, gfin_ref, o_ref, xa_ref):
    npair = d // LANES
    mod = mod_ref[0]
    gt1 = mod[2:3]
    sh2 = mod[3:4]
    sc2 = mod[4:5]
    gt2 = mod[5:6]
    m0 = lax.broadcasted_iota(jnp.int32, (1, LANES), 1) < HEAD_DIM
    inv_n = 1.0 / HEAD_DIM
    for p in range(npair):
        sl = slice(p * LANES, (p + 1) * LANES)
        wkv = yf_ref[p] + yb_ref[p]
        cen = wkv - _head_sum(wkv, m0) * inv_n
        var = _head_sum(cen * cen, m0) * inv_n
        ln = cen * lax.rsqrt(var + LNX_EPS) * lnw_ref[:, sl] + lnb_ref[:, sl]
        xa_ref[:, sl] = ((ln + bonus_ref[:, sl]) * gout_ref[:, sl]).astype(BF16)
    ya = _dot(xa_ref[...], wpa_ref[...])
    merged = ga_ref[...] * ya + gbyb_ref[...]
    x1 = x_ref[...] + gt1 * _dot(merged.astype(BF16), wo_ref[...])
    xn2 = (_rms(x1) * g2n_ref[...] * (1.0 + sc2) + sh2).astype(BF16)
    hid = jnp.maximum(_dot(xn2, wff1_ref[...]), 0.0)
    x2 = x1 + gt2 * _dot((hid * hid).astype(BF16), wff2_ref[...])
    o_ref[...] = _rms(x2) * gfin_ref[...]


def _post(yf, yb, gout, bonus, ga, gbyb, x, mod, tokens_per_mod, wts):
    ntok, d = x.shape
    tm = TOKEN_TILE
    npair = d // LANES
    flat_spec = pl.BlockSpec((tm, d), lambda i: (i, 0))
    pm_spec = pl.BlockSpec((npair, tm, LANES), lambda i: (0, i, 0))
    names = ("lnx_w", "lnx_b", "w_pa", "w_o", "g2n", "w_ff1", "w_ff2", "g_fin")
    return pl.pallas_call(
        functools.partial(_post_kernel, d),
        grid=(ntok // tm,),
        in_specs=[pm_spec, pm_spec,
                  flat_spec, flat_spec, flat_spec, flat_spec, flat_spec,
                  pl.BlockSpec((1,) + mod.shape[1:], lambda i: ((i * tm) // tokens_per_mod, 0, 0))]
                 + [_resident(wts[n].shape) for n in names],
        out_specs=flat_spec,
        out_shape=jax.ShapeDtypeStruct((ntok, d), F32),
        scratch_shapes=[pltpu.VMEM((tm, d), BF16)],
        compiler_params=pltpu.CompilerParams(dimension_semantics=("parallel",), vmem_limit_bytes=VMEM_LIMIT),
        name="post_mlp",
    )(yf, yb, gout, bonus, ga, gbyb, x, mod, *[wts[n] for n in names])


def _state_to_pairs(s):
    b, h, n, _ = s.shape
    x = s.reshape(b, h // PAIR, PAIR, n, n)
    return jnp.transpose(x, (0, 1, 4, 2, 3)).reshape(b, h // PAIR, n, PAIR * n)


def _pairs_to_state(t):
    b, npair, n, _ = t.shape
    x = t.reshape(b, npair, n, PAIR, n)
    return jnp.transpose(x, (0, 1, 3, 4, 2)).reshape(b, npair * PAIR, n, n)


def _layer(x3, mod, s0_f, s0_b, seg, wts):
    batch, t, d = x3.shape
    x = x3.reshape(batch * t, d)
    tokens_per_mod = t if mod.shape[0] == batch else batch * t
    r, k, v, a, lw, gout, bonus, ga, gbyb = _in_proj(x, mod, tokens_per_mod, seg, wts)
    s0 = jnp.stack([_state_to_pairs(s0_f), _state_to_pairs(s0_b)], axis=0)
    yf, yb, sfin = _scan(r, k, v, a, lw, wts["k_k_pm"], wts["k_a_pm"], s0, batch)
    out = _post(yf, yb, gout, bonus, ga, gbyb, x, mod, tokens_per_mod, wts)
    return out.reshape(batch, t, d), _pairs_to_state(sfin[0]), _pairs_to_state(sfin[1])


def kernel(x_prompt, x_sample, state_rwkv_fwd, state_rwkv_bwd, c, c_ctx, norm1_g, norm2_g, w_ada, b_ada, w_in, w0, w2, a0, a2, g2, k_k, k_a, r_k, lnx_w, lnx_b, conv_w, w_pa, w_pb, w_o, w_ff1, w_ff2, final_norm_g):
    depth, d = norm1_g.shape
    assert depth == 1, "the fused final norm assumes a single layer"
    n_heads = r_k.shape[1]
    assert n_heads * HEAD_DIM == d and r_k.shape[2] == HEAD_DIM
    npair = d // LANES
    lora = w2.shape[2]
    assert 2 * lora == LANES and g2.shape[1] == LANES and a2.shape[2] == lora

    def blockdiag2(w):
        z = jnp.zeros_like(w[0])
        return jnp.concatenate([jnp.concatenate([w[0], z], axis=1), jnp.concatenate([z, w[1]], axis=1)], axis=0)

    wts = {
        "g1": norm1_g, "w_in": w_in[0].astype(BF16),
        "w2cat": blockdiag2(w2[0]).astype(BF16), "w0cat": w0[0].reshape(1, 2 * d),
        "a2cat": blockdiag2(a2[0]).astype(BF16), "a0cat": a0[0].reshape(1, 2 * d),
        "g2": g2[0].astype(BF16), "conv_w": conv_w[0], "w_pb": w_pb[0].astype(BF16),
        "k_a": k_a, "r_k": r_k[0].reshape(1, d),
        "k_k_pm": k_k[0].reshape(npair, 1, LANES), "k_a_pm": k_a[0].reshape(npair, 1, LANES),
        "lnx_w": lnx_w, "lnx_b": lnx_b, "w_pa": w_pa[0].astype(BF16), "w_o": w_o[0].astype(BF16),
        "g2n": norm2_g, "w_ff1": w_ff1[0].astype(BF16), "w_ff2": w_ff2[0].astype(BF16),
        "g_fin": final_norm_g.reshape(1, d),
    }

    n_lat = c.shape[0]
    rows = -(-(1 + n_lat) // 8) * 8
    cv = jnp.concatenate([c_ctx[None, :], c, jnp.zeros((rows - 1 - n_lat, d), F32)], axis=0)
    mod = _ada(cv, w_ada[0], b_ada).reshape(rows, 6, d)

    bp = x_prompt.shape[0]
    zero_state = jnp.zeros((bp, n_heads, HEAD_DIM, HEAD_DIM), F32)
    y_prompt, sf, sb = _layer(x_prompt, mod[0:1], zero_state, zero_state, x_prompt.shape[1], wts)
    y_sample, _, _ = _layer(x_sample, mod[1:1 + n_lat], state_rwkv_fwd[:, 0], state_rwkv_bwd[:, 0], GRID_W, wts)
    return (y_prompt, y_sample, sf[:, None], sb[:, None])
```

```python
import functools

import jax
import jax.numpy as jnp
from jax import lax
from jax.experimental import pallas as pl
from jax.experimental.pallas import tpu as pltpu

F32 = jnp.float32
BF16 = jnp.bfloat16

HEAD_DIM = 64
LANES = 128
PAIR = LANES // HEAD_DIM
GRID_W = 64
EPS = 1e-6
LNX_EPS = 64e-5
CHUNK = 64
TOKEN_TILE = 256
SCAN_SUBCHUNKS = 2
DIR_PAIR_GROUP = 16
VMEM_LIMIT = 58 * 1024 * 1024


def _resident(shape):
    nd = len(shape)
    return pl.BlockSpec(shape, lambda *_: (0,) * nd, pipeline_mode=pl.Buffered(1))


def _dot(a, b):
    return jnp.dot(a, b, preferred_element_type=F32)


def _softplus(x):
    return jnp.maximum(x, 0.0) + jnp.log1p(jnp.exp(-jnp.abs(x)))


def _head_sum(x, m0):
    lo = jnp.sum(jnp.where(m0, x, 0.0), axis=-1, keepdims=True)
    hi = jnp.sum(jnp.where(m0, 0.0, x), axis=-1, keepdims=True)
    return jnp.where(m0, lo, hi)


def _rms(x):
    return x * lax.rsqrt(jnp.mean(x * x, axis=-1, keepdims=True) + EPS)


def _ada_kernel(c_ref, w_ref, b_ref, o_ref):
    c = c_ref[...]
    s = (c * jax.nn.sigmoid(c)).astype(BF16)
    o_ref[...] = _dot(s, w_ref[...].astype(BF16)) + b_ref[...]


def _ada(cv, w_ada, b_ada):
    rows, d = cv.shape
    n = w_ada.shape[1]
    tn = d
    return pl.pallas_call(
        _ada_kernel,
        grid=(n // tn,),
        in_specs=[
            pl.BlockSpec((rows, d), lambda j: (0, 0)),
            pl.BlockSpec((d, tn), lambda j: (0, j)),
            pl.BlockSpec((1, tn), lambda j: (0, j)),
        ],
        out_specs=pl.BlockSpec((rows, tn), lambda j: (0, j)),
        out_shape=jax.ShapeDtypeStruct((rows, n), F32),
        name="ada_mod",
    )(cv, w_ada, b_ada)


def _in_proj_kernel(seg, d, x_ref, mod_ref, g1_ref, win_ref, w2_ref, w0_ref, a2_ref, a0_ref, g2_ref,
                    cw_ref, wpb_ref, ka_ref, rk_ref,
                    r_ref, k_ref, v_ref, a_ref, lw_ref, gout_ref, bonus_ref, ga_ref, gbyb_ref):
    tm = x_ref.shape[0]
    npair = d // LANES
    mod = mod_ref[0]
    sh1 = mod[0:1]
    sc1 = mod[1:2]
    xn = _rms(x_ref[...]) * g1_ref[...]
    xn = (xn * (1.0 + sc1) + sh1).astype(BF16)

    def proj(lo, hi):
        return _dot(xn, win_ref[:, lo:hi])

    o_lora = 3 * d
    zl = proj(o_lora, o_lora + 3 * LANES)
    wl = zl[:, 0:LANES]
    al = zl[:, LANES:2 * LANES]
    gl = zl[:, 2 * LANES:3 * LANES]
    pre_w = _dot(jnp.tanh(wl).astype(BF16), w2_ref[...]) + w0_ref[...]
    lw = -jnp.exp(-_softplus(-pre_w) - 0.5)
    a = jax.nn.sigmoid(_dot(al.astype(BF16), a2_ref[...]) + a0_ref[...])
    gout_ref[...] = _dot(jax.nn.sigmoid(gl).astype(BF16), g2_ref[...]).astype(BF16)

    r = proj(0, d)
    k = proj(d, 2 * d)
    v = proj(2 * d, 3 * d)
    q = r * rk_ref[...] * k * (2.0 + (a[:, :d] + a[:, d:] - 2.0) * ka_ref[...])
    m0 = lax.broadcasted_iota(jnp.int32, (1, LANES), 1) < HEAD_DIM
    for p in range(npair):
        sl = slice(p * LANES, (p + 1) * LANES)
        r_ref[p] = r[:, sl]
        k_ref[p] = k[:, sl]
        v_ref[p] = v[:, sl].astype(BF16)
        for dr in range(2):
            sd = slice(dr * d + p * LANES, dr * d + (p + 1) * LANES)
            a_ref[dr, p] = a[:, sd]
            lw_ref[dr, p] = lw[:, sd]
        bonus_ref[:, sl] = (v[:, sl] * _head_sum(q[:, sl], m0)).astype(BF16)

    o_conv = o_lora + 3 * LANES
    cb = proj(o_conv, o_conv + d)
    u = proj(o_conv + d, o_conv + 2 * d) * proj(o_conv + 2 * d, o_conv + 3 * d)
    pos = lax.broadcasted_iota(jnp.int32, (tm, 1), 0) % seg
    u_prev = jnp.where(pos == 0, 0.0, pltpu.roll(u, 1, 0))
    u_next = jnp.where(pos == seg - 1, 0.0, pltpu.roll(u, tm - 1, 0))
    conv = cw_ref[0:1] * u_prev + cw_ref[1:2] * u + cw_ref[2:3] * u_next
    yb = _dot((cb * conv).astype(BF16), wpb_ref[...])

    o_gate = o_conv + 3 * d
    ga_ref[...] = jax.nn.sigmoid(proj(o_gate, o_gate + d)).astype(BF16)
    gbyb_ref[...] = (jax.nn.sigmoid(proj(o_gate + d, o_gate + 2 * d)) * yb).astype(BF16)


def _in_proj(x, mod, tokens_per_mod, seg, wts):
    ntok, d = x.shape
    tm = TOKEN_TILE
    npair = d // LANES
    flat = jax.ShapeDtypeStruct((ntok, d), F32)
    pm = jax.ShapeDtypeStruct((npair, ntok, LANES), F32)
    pm2 = jax.ShapeDtypeStruct((2, npair, ntok, LANES), F32)
    flat_b = jax.ShapeDtypeStruct((ntok, d), BF16)
    pm_b = jax.ShapeDtypeStruct((npair, ntok, LANES), BF16)
    flat_spec = pl.BlockSpec((tm, d), lambda i: (i, 0))
    pm_spec = pl.BlockSpec((npair, tm, LANES), lambda i: (0, i, 0))
    pm2_spec = pl.BlockSpec((2, npair, tm, LANES), lambda i: (0, 0, i, 0))
    names = ("g1", "w_in", "w2cat", "w0cat", "a2cat", "a0cat", "g2", "conv_w", "w_pb", "k_a", "r_k")
    return pl.pallas_call(
        functools.partial(_in_proj_kernel, seg, d),
        grid=(ntok // tm,),
        in_specs=[flat_spec,
                  pl.BlockSpec((1,) + mod.shape[1:], lambda i: ((i * tm) // tokens_per_mod, 0, 0))]
                 + [_resident(wts[n].shape) for n in names],
        out_specs=[pm_spec, pm_spec, pm_spec, pm2_spec, pm2_spec, flat_spec, flat_spec, flat_spec, flat_spec],
        out_shape=[pm, pm, pm_b, pm2, pm2, flat_b, flat_b, flat_b, flat_b],
        compiler_params=pltpu.CompilerParams(dimension_semantics=("parallel",), vmem_limit_bytes=VMEM_LIMIT),
        name="in_proj",
    )(x, mod, *[wts[n] for n in names])


def _scan_kernel(n_chunks, rf_ref, kf_ref, vf_ref, rb_ref, kb_ref, vb_ref, af_ref, lwf_ref, ab_ref, lwb_ref,
                 kkw_ref, ka_ref, s0_ref, yf_ref, yb_ref, sfin_ref, h_ref):
    c = pl.program_id(1)
    npair = rf_ref.shape[0]
    cl = CHUNK
    nsub = rf_ref.shape[1] // cl
    r_refs, k_refs, v_refs = (rf_ref, rb_ref), (kf_ref, kb_ref), (vf_ref, vb_ref)
    a_refs, lw_refs, y_refs = (af_ref, ab_ref), (lwf_ref, lwb_ref), (yf_ref, yb_ref)

    @pl.when(c == 0)
    def _():
        h_ref[...] = s0_ref[:, 0]

    row = lax.broadcasted_iota(jnp.int32, (cl, LANES), 0)
    col = lax.broadcasted_iota(jnp.int32, (cl, LANES), 1) % HEAD_DIM
    dt = row - col
    mask_s = (dt > 0, dt < 0)
    mask_i = (dt >= 0, dt <= 0)
    eye = (row == col).astype(F32)
    tri = tuple(m[:, :cl].astype(BF16) for m in mask_i)
    m0 = lax.broadcasted_iota(jnp.int32, (1, LANES), 1) < HEAD_DIM

    def diag2(x):
        return jnp.concatenate([jnp.where(m0, x, 0.0), jnp.where(m0, 0.0, x)], axis=0)

    def side_by_side_t(x):
        t = jnp.transpose(diag2(x))
        return t[:cl] + t[cl:]

    def each(f, *lists):
        return [f(*xs) for xs in zip(*lists)]

    def group_body(chains):
        r = [r_refs[dr][p, sub * cl:(sub + 1) * cl] for dr, p, sub in chains]
        k = [k_refs[dr][p, sub * cl:(sub + 1) * cl] for dr, p, sub in chains]
        v = [v_refs[dr][p, sub * cl:(sub + 1) * cl] for dr, p, sub in chains]
        a = [a_refs[dr][0, p, sub * cl:(sub + 1) * cl] for dr, p, sub in chains]
        lw = [lw_refs[dr][0, p, sub * cl:(sub + 1) * cl] for dr, p, sub in chains]
        ms = [mask_s[dr] for dr, _, _ in chains]
        mi = [mask_i[dr] for dr, _, _ in chains]
        kkr = [x * kkw_ref[p] for x, (_, p, _) in zip(k, chains)]
        kk = each(lambda x: x * lax.rsqrt(jnp.maximum(_head_sum(x * x, m0), 1e-12)), kkr)
        kd = [x * (1.0 + (y - 1.0) * ka_ref[p]) for x, y, (_, p, _) in zip(k, a, chains)]
        bv = each(lambda x, y: x * y, kk, a)

        def cumsum(x, tri_d):
            hi = x.astype(BF16)
            lo = (x - hi.astype(F32)).astype(BF16)
            cs2 = _dot(tri_d, jnp.concatenate([hi, lo], axis=1))
            return cs2[:, :LANES] + cs2[:, LANES:]

        cs = each(cumsum, lw, [tri[dr] for dr, _, _ in chains])
        tot = each(lambda x: jnp.sum(x, axis=0, keepdims=True), lw)
        e_neg = each(lambda x: jnp.exp(-x), cs)
        e_rem = each(lambda t, x: jnp.exp(t - x), tot, cs)
        rt = each(lambda x, y: x * jnp.exp(y), r, cs)
        at = each(lambda x, y, z: (-x * jnp.exp(y - z)).astype(BF16), kk, cs, lw)
        kt_d = each(lambda x, y: diag2((x * y).astype(BF16)), kd, e_neg)
        bt_d = each(lambda x, y: diag2((x * y).astype(BF16)), bv, e_neg)
        kb_t = each(lambda x, y, z: jnp.concatenate([side_by_side_t(x * z), side_by_side_t(y * z)], axis=1)
                    .astype(BF16), kd, bv, e_rem)
        v_d = each(lambda x: diag2(x.astype(BF16)), v)

        def scores(at_, rt_, bt_, kt_):
            lhs = jnp.concatenate([at_, rt_.astype(BF16)], axis=0)
            rhs = jnp.concatenate([bt_, kt_], axis=0)
            return lax.dot_general(lhs, rhs, (((1,), (1,)), ((), ())), preferred_element_type=F32)

        o1 = each(scores, at, rt, bt_d, kt_d)
        a_ab = each(lambda o, m: jnp.where(m, o[:cl, :LANES], 0.0), o1, ms)
        a_ak = each(lambda o, m: jnp.where(m, o[:cl, LANES:], 0.0).astype(BF16), o1, ms)
        a_r = each(lambda o, m: jnp.concatenate([jnp.where(m, o[cl:, LANES:], 0.0),
                                                  jnp.where(m, o[cl:, :LANES], 0.0)], axis=1).astype(BF16), o1, mi)

        t_inv = each(lambda x: eye + x, a_ab)
        npow = each(lambda x: x.astype(BF16), a_ab)
        npow = each(lambda x: _dot(x, diag2(x)).astype(BF16), npow)
        for _ in range(cl.bit_length() - 3):
            both = each(lambda n, t: _dot(n, jnp.concatenate([diag2(n), diag2(t.astype(BF16))], axis=1)),
                        npow, t_inv)
            npow = each(lambda x: x[:, :LANES].astype(BF16), both)
            t_inv = each(lambda t, x: t + x[:, LANES:], t_inv, both)
        t_inv = each(lambda t, n: t + _dot(n, diag2(t.astype(BF16))), t_inv, npow)

        x1 = each(lambda x, y: _dot(x, y).astype(BF16), a_ak, v_d)
        ut = each(lambda t, x, y: _dot(t.astype(BF16), jnp.concatenate([diag2(x), diag2(y)], axis=1))
                  .astype(BF16), t_inv, x1, at)
        rhs2 = each(lambda x, u: jnp.concatenate(
            [jnp.concatenate([x, jnp.zeros_like(x)], axis=1),
             jnp.concatenate([diag2(u[:, :LANES]), diag2(u[:, LANES:])], axis=1)], axis=0), v_d, ut)
        o2 = each(lambda x, y, z: _dot(jnp.concatenate([x, y], axis=0), z), a_r, kb_t, rhs2)

        def decay_col(t):
            gc = jnp.transpose(jnp.broadcast_to(jnp.exp(t), (LANES, LANES)))
            return jnp.where(m0, gc[:cl], gc[cl:])

        return dict(zip(chains, zip(o2, rt, each(decay_col, tot))))

    def state_stage(local, dps, step):
        subs = [step if dr == 0 else nsub - 1 - step for dr, _ in dps]
        o2, rt, dec = zip(*[local[(dr, p, sub)] for (dr, p), sub in zip(dps, subs)])
        h = [h_ref[dr, p] for dr, p in dps]
        h_d = each(lambda x: diag2(x.astype(BF16)), h)
        y = each(lambda o, rt_, x: o[:cl, :LANES] + _dot((rt_ + o[:cl, LANES:]).astype(BF16), x), o2, rt, h_d)
        h_new = each(lambda g, x, o, xd: g * x + _dot(o[cl:, LANES:].astype(BF16), xd) + o[cl:, :LANES],
                     dec, h, o2, h_d)
        for (dr, p), sub, y_p, hn in zip(dps, subs, y, h_new):
            y_refs[dr][p, sub * cl:(sub + 1) * cl] = y_p.astype(BF16)
            h_ref[dr, p] = hn

    dir_pairs = [(dr, p) for p in range(npair) for dr in range(2)]
    for g0 in range(0, len(dir_pairs), DIR_PAIR_GROUP):
        dps = dir_pairs[g0:g0 + DIR_PAIR_GROUP]
        local = group_body([(dr, p, sub) for sub in range(nsub) for dr, p in dps])
        for step in range(nsub):
            state_stage(local, dps, step)

    @pl.when(c == n_chunks - 1)
    def _():
        sfin_ref[:, 0] = h_ref[...]


def _scan(r, k, v, a, lw, kkw, ka, s0, batch):
    npair, ntok, _ = r.shape
    assert CHUNK == HEAD_DIM, "the side-by-side tiles assume chunk length == head dim"
    cl = SCAN_SUBCHUNKS * CHUNK
    n_chunks = ntok // batch // cl

    def fwd(b, c):
        return b * n_chunks + c

    def bwd(b, c):
        return b * n_chunks + n_chunks - 1 - c

    def shared(idx):
        return pl.BlockSpec((npair, cl, LANES), lambda b, c: (0, idx(b, c), 0))

    def per_dir(dr, idx):
        return pl.BlockSpec((1, npair, cl, LANES), lambda b, c: (dr, 0, idx(b, c), 0))

    state = pl.BlockSpec((2, 1, npair, HEAD_DIM, LANES), lambda b, c: (0, b, 0, 0, 0))
    y_shape = jax.ShapeDtypeStruct((npair, ntok, LANES), BF16)
    return pl.pallas_call(
        functools.partial(_scan_kernel, n_chunks),
        grid=(batch, n_chunks),
        in_specs=[shared(fwd), shared(fwd), shared(fwd), shared(bwd), shared(bwd), shared(bwd),
                  per_dir(0, fwd), per_dir(0, fwd), per_dir(1, bwd), per_dir(1, bwd),
                  _resident(kkw.shape), _resident(ka.shape), state],
        out_specs=[shared(fwd), shared(bwd), state],
        out_shape=[y_shape, y_shape, jax.ShapeDtypeStruct(s0.shape, F32)],
        scratch_shapes=[pltpu.VMEM((2, npair, HEAD_DIM, LANES), F32)],
        compiler_params=pltpu.CompilerParams(
            dimension_semantics=("parallel", "arbitrary"), vmem_limit_bytes=VMEM_LIMIT),
        name="wkv_scan",
    )(r, k, v, r, k, v, a, lw, a, lw, kkw, ka, s0)


def _post_kernel(d, yf_ref, yb_ref, gout_ref, bonus_ref, ga_ref, gbyb_ref, x_ref, mod_ref, lnw_ref, lnb_ref,
                 wpa_ref, wo_ref, g2n_ref, wff1_ref, wff2_ref, gfin_ref, o_ref, xa_ref):
    npair = d // LANES
    mod = mod_ref[0]
    gt1 = mod[2:3]
    sh2 = mod[3:4]
    sc2 = mod[4:5]
    gt2 = mod[5:6]
    m0 = lax.broadcasted_iota(jnp.int32, (1, LANES), 1) < HEAD_DIM
    inv_n = 1.0 / HEAD_DIM
    for p in range(npair):
        sl = slice(p * LANES, (p + 1) * LANES)
        wkv = yf_ref[p].astype(F32) + yb_ref[p].astype(F32)
        cen = wkv - _head_sum(wkv, m0) * inv_n
        var = _head_sum(cen * cen, m0) * inv_n
        ln = cen * lax.rsqrt(var + LNX_EPS) * lnw_ref[:, sl] + lnb_ref[:, sl]
        xa_ref[:, sl] = ((ln + bonus_ref[:, sl].astype(F32)) * gout_ref[:, sl].astype(F32)).astype(BF16)
    ya = _dot(xa_ref[...], wpa_ref[...])
    merged = ga_ref[...].astype(F32) * ya + gbyb_ref[...].astype(F32)
    x1 = x_ref[...] + gt1 * _dot(merged.astype(BF16), wo_ref[...])
    xn2 = (_rms(x1) * g2n_ref[...] * (1.0 + sc2) + sh2).astype(BF16)
    hid = jnp.maximum(_dot(xn2, wff1_ref[...]), 0.0)
    x2 = x1 + gt2 * _dot((hid * hid).astype(BF16), wff2_ref[...])
    o_ref[...] = _rms(x2) * gfin_ref[...]


def _post(yf, yb, gout, bonus, ga, gbyb, x, mod, tokens_per_mod, wts):
    ntok, d = x.shape
    tm = TOKEN_TILE
    npair = d // LANES
    flat_spec = pl.BlockSpec((tm, d), lambda i: (i, 0))
    pm_spec = pl.BlockSpec((npair, tm, LANES), lambda i: (0, i, 0))
    names = ("lnx_w", "lnx_b", "w_pa", "w_o", "g2n", "w_ff1", "w_ff2", "g_fin")
    return pl.pallas_call(
        functools.partial(_post_kernel, d),
        grid=(ntok // tm,),
        in_specs=[pm_spec, pm_spec,
                  flat_spec, flat_spec, flat_spec, flat_spec, flat_spec,
                  pl.BlockSpec((1,) + mod.shape[1:], lambda i: ((i * tm) // tokens_per_mod, 0, 0))]
                 + [_resident(wts[n].shape) for n in names],
        out_specs=flat_spec,
        out_shape=jax.ShapeDtypeStruct((ntok, d), F32),
        scratch_shapes=[pltpu.VMEM((tm, d), BF16)],
        compiler_params=pltpu.CompilerParams(dimension_semantics=("parallel",), vmem_limit_bytes=VMEM_LIMIT),
        name="post_mlp",
    )(yf, yb, gout, bonus, ga, gbyb, x, mod, *[wts[n] for n in names])


def _state_to_pairs(s):
    b, h, n, _ = s.shape
    x = s.reshape(b, h // PAIR, PAIR, n, n)
    return jnp.transpose(x, (0, 1, 4, 2, 3)).reshape(b, h // PAIR, n, PAIR * n)


def _pairs_to_state(t):
    b, npair, n, _ = t.shape
    x = t.reshape(b, npair, n, PAIR, n)
    return jnp.transpose(x, (0, 1, 3, 4, 2)).reshape(b, npair * PAIR, n, n)


def _layer(x3, mod, s0_f, s0_b, seg, wts):
    batch, t, d = x3.shape
    x = x3.reshape(batch * t, d)
    tokens_per_mod = t if mod.shape[0] == batch else batch * t
    r, k, v, a, lw, gout, bonus, ga, gbyb = _in_proj(x, mod, tokens_per_mod, seg, wts)
    s0 = jnp.stack([_state_to_pairs(s0_f), _state_to_pairs(s0_b)], axis=0)
    yf, yb, sfin = _scan(r, k, v, a, lw, wts["k_k_pm"], wts["k_a_pm"], s0, batch)
    out = _post(yf, yb, gout, bonus, ga, gbyb, x, mod, tokens_per_mod, wts)
    return out.reshape(batch, t, d), _pairs_to_state(sfin[0]), _pairs_to_state(sfin[1])


def kernel(x_prompt, x_sample, state_rwkv_fwd, state_rwkv_bwd, c, c_ctx, norm1_g, norm2_g, w_ada, b_ada, w_in, w0, w2, a0, a2, g2, k_k, k_a, r_k, lnx_w, lnx_b, conv_w, w_pa, w_pb, w_o, w_ff1, w_ff2, final_norm_g):
    depth, d = norm1_g.shape
    assert depth == 1, "the fused final norm assumes a single layer"
    n_heads = r_k.shape[1]
    assert n_heads * HEAD_DIM == d and r_k.shape[2] == HEAD_DIM
    npair = d // LANES
    lora = w2.shape[2]
    assert 2 * lora == LANES and g2.shape[1] == LANES and a2.shape[2] == lora

    def blockdiag2(w):
        z = jnp.zeros_like(w[0])
        return jnp.concatenate([jnp.concatenate([w[0], z], axis=1), jnp.concatenate([z, w[1]], axis=1)], axis=0)

    wts = {
        "g1": norm1_g, "w_in": w_in[0].astype(BF16),
        "w2cat": blockdiag2(w2[0]).astype(BF16), "w0cat": w0[0].reshape(1, 2 * d),
        "a2cat": blockdiag2(a2[0]).astype(BF16), "a0cat": a0[0].reshape(1, 2 * d),
        "g2": g2[0].astype(BF16), "conv_w": conv_w[0], "w_pb": w_pb[0].astype(BF16),
        "k_a": k_a, "r_k": r_k[0].reshape(1, d),
        "k_k_pm": k_k[0].reshape(npair, 1, LANES), "k_a_pm": k_a[0].reshape(npair, 1, LANES),
        "lnx_w": lnx_w, "lnx_b": lnx_b, "w_pa": w_pa[0].astype(BF16), "w_o": w_o[0].astype(BF16),
        "g2n": norm2_g, "w_ff1": w_ff1[0].astype(BF16), "w_ff2": w_ff2[0].astype(BF16),
        "g_fin": final_norm_g.reshape(1, d),
    }

    n_lat = c.shape[0]
    rows = -(-(1 + n_lat) // 8) * 8
    cv = jnp.concatenate([c_ctx[None, :], c, jnp.zeros((rows - 1 - n_lat, d), F32)], axis=0)
    mod = _ada(cv, w_ada[0], b_ada).reshape(rows, 6, d)

    bp = x_prompt.shape[0]
    zero_state = jnp.zeros((bp, n_heads, HEAD_DIM, HEAD_DIM), F32)
    y_prompt, sf, sb = _layer(x_prompt, mod[0:1], zero_state, zero_state, x_prompt.shape[1], wts)
    y_sample, _, _ = _layer(x_sample, mod[1:1 + n_lat], state_rwkv_fwd[:, 0], state_rwkv_bwd[:, 0], GRID_W, wts)
    return (y_prompt, y_sample, sf[:, None], sb[:, None])
```

```python
import functools

import jax
import jax.numpy as jnp
from jax import lax
from jax.experimental import pallas as pl
from jax.experimental.pallas import tpu as pltpu

F32 = jnp.float32
BF16 = jnp.bfloat16

HEAD_DIM = 64
LANES = 128
PAIR = LANES // HEAD_DIM
GRID_W = 64
EPS = 1e-6
LNX_EPS = 64e-5
CHUNK = 64
TOKEN_TILE = 256
POST_TOKEN_TILE = 512
SCAN_SUBCHUNKS = 2
DIR_PAIR_GROUP = 16
VMEM_LIMIT = 58 * 1024 * 1024


def _resident(shape):
    nd = len(shape)
    return pl.BlockSpec(shape, lambda *_: (0,) * nd, pipeline_mode=pl.Buffered(1))


def _dot(a, b):
    return jnp.dot(a, b, preferred_element_type=F32)


def _softplus(x):
    return jnp.maximum(x, 0.0) + jnp.log1p(jnp.exp(-jnp.abs(x)))


def _head_sum(x, m0):
    lo = jnp.sum(jnp.where(m0, x, 0.0), axis=-1, keepdims=True)
    hi = jnp.sum(jnp.where(m0, 0.0, x), axis=-1, keepdims=True)
    return jnp.where(m0, lo, hi)


def _rms(x):
    return x * lax.rsqrt(jnp.mean(x * x, axis=-1, keepdims=True) + EPS)


def _ada_kernel(c_ref, w_ref, b_ref, o_ref):
    c = c_ref[...]
    s = (c * jax.nn.sigmoid(c)).astype(BF16)
    o_ref[...] = _dot(s, w_ref[...].astype(BF16)) + b_ref[...]


def _ada(cv, w_ada, b_ada):
    rows, d = cv.shape
    n = w_ada.shape[1]
    tn = d
    return pl.pallas_call(
        _ada_kernel,
        grid=(n // tn,),
        in_specs=[
            pl.BlockSpec((rows, d), lambda j: (0, 0)),
            pl.BlockSpec((d, tn), lambda j: (0, j)),
            pl.BlockSpec((1, tn), lambda j: (0, j)),
        ],
        out_specs=pl.BlockSpec((rows, tn), lambda j: (0, j)),
        out_shape=jax.ShapeDtypeStruct((rows, n), F32),
        name="ada_mod",
    )(cv, w_ada, b_ada)


def _in_proj_kernel(seg, d, x_ref, mod_ref, g1_ref, win_ref, w2_ref, w0_ref, a2_ref, a0_ref, g2_ref,
                    cw_ref, wpb_ref, ka_ref, rk_ref,
                    r_ref, k_ref, v_ref, a_ref, lw_ref, gout_ref, bonus_ref, ga_ref, gbyb_ref):
    tm = x_ref.shape[0]
    npair = d // LANES
    mod = mod_ref[0]
    sh1 = mod[0:1]
    sc1 = mod[1:2]
    xn = _rms(x_ref[...]) * g1_ref[...]
    xn = (xn * (1.0 + sc1) + sh1).astype(BF16)

    def proj(lo, hi):
        return _dot(xn, win_ref[:, lo:hi])

    o_lora = 3 * d
    zl = proj(o_lora, o_lora + 3 * LANES)
    wl = zl[:, 0:LANES]
    al = zl[:, LANES:2 * LANES]
    gl = zl[:, 2 * LANES:3 * LANES]
    pre_w = _dot(jnp.tanh(wl).astype(BF16), w2_ref[...]) + w0_ref[...]
    lw = -jnp.exp(-_softplus(-pre_w) - 0.5)
    a = jax.nn.sigmoid(_dot(al.astype(BF16), a2_ref[...]) + a0_ref[...])
    gout_ref[...] = _dot(jax.nn.sigmoid(gl).astype(BF16), g2_ref[...]).astype(BF16)

    r = proj(0, d)
    k = proj(d, 2 * d)
    v = proj(2 * d, 3 * d)
    q = r * rk_ref[...] * k * (2.0 + (a[:, :d] + a[:, d:] - 2.0) * ka_ref[...])
    m0 = lax.broadcasted_iota(jnp.int32, (1, LANES), 1) < HEAD_DIM
    for p in range(npair):
        sl = slice(p * LANES, (p + 1) * LANES)
        r_ref[p] = r[:, sl]
        k_ref[p] = k[:, sl]
        v_ref[p] = v[:, sl].astype(BF16)
        for dr in range(2):
            sd = slice(dr * d + p * LANES, dr * d + (p + 1) * LANES)
            a_ref[dr, p] = a[:, sd]
            lw_ref[dr, p] = lw[:, sd]
        bonus_ref[:, sl] = (v[:, sl] * _head_sum(q[:, sl], m0)).astype(BF16)

    o_conv = o_lora + 3 * LANES
    cb = proj(o_conv, o_conv + d)
    u = proj(o_conv + d, o_conv + 2 * d) * proj(o_conv + 2 * d, o_conv + 3 * d)
    pos = lax.broadcasted_iota(jnp.int32, (tm, 1), 0) % seg
    u_prev = jnp.where(pos == 0, 0.0, pltpu.roll(u, 1, 0))
    u_next = jnp.where(pos == seg - 1, 0.0, pltpu.roll(u, tm - 1, 0))
    conv = cw_ref[0:1] * u_prev + cw_ref[1:2] * u + cw_ref[2:3] * u_next
    yb = _dot((cb * conv).astype(BF16), wpb_ref[...])

    o_gate = o_conv + 3 * d
    ga_ref[...] = jax.nn.sigmoid(proj(o_gate, o_gate + d)).astype(BF16)
    gbyb_ref[...] = (jax.nn.sigmoid(proj(o_gate + d, o_gate + 2 * d)) * yb).astype(BF16)


def _in_proj(x, mod, tokens_per_mod, seg, wts):
    ntok, d = x.shape
    tm = TOKEN_TILE
    npair = d // LANES
    flat = jax.ShapeDtypeStruct((ntok, d), F32)
    pm = jax.ShapeDtypeStruct((npair, ntok, LANES), F32)
    pm2 = jax.ShapeDtypeStruct((2, npair, ntok, LANES), F32)
    flat_b = jax.ShapeDtypeStruct((ntok, d), BF16)
    pm_b = jax.ShapeDtypeStruct((npair, ntok, LANES), BF16)
    flat_spec = pl.BlockSpec((tm, d), lambda i: (i, 0))
    pm_spec = pl.BlockSpec((npair, tm, LANES), lambda i: (0, i, 0))
    pm2_spec = pl.BlockSpec((2, npair, tm, LANES), lambda i: (0, 0, i, 0))
    names = ("g1", "w_in", "w2cat", "w0cat", "a2cat", "a0cat", "g2", "conv_w", "w_pb", "k_a", "r_k")
    return pl.pallas_call(
        functools.partial(_in_proj_kernel, seg, d),
        grid=(ntok // tm,),
        in_specs=[flat_spec,
                  pl.BlockSpec((1,) + mod.shape[1:], lambda i: ((i * tm) // tokens_per_mod, 0, 0))]
                 + [_resident(wts[n].shape) for n in names],
        out_specs=[pm_spec, pm_spec, pm_spec, pm2_spec, pm2_spec, flat_spec, flat_spec, flat_spec, flat_spec],
        out_shape=[pm, pm, pm_b, pm2, pm2, flat_b, flat_b, flat_b, flat_b],
        compiler_params=pltpu.CompilerParams(dimension_semantics=("parallel",), vmem_limit_bytes=VMEM_LIMIT),
        name="in_proj",
    )(x, mod, *[wts[n] for n in names])


def _scan_kernel(n_chunks, rf_ref, kf_ref, vf_ref, rb_ref, kb_ref, vb_ref, af_ref, lwf_ref, ab_ref, lwb_ref,
                 kkw_ref, ka_ref, s0_ref, yf_ref, yb_ref, sfin_ref, h_ref):
    c = pl.program_id(1)
    npair = rf_ref.shape[0]
    cl = CHUNK
    nsub = rf_ref.shape[1] // cl
    r_refs, k_refs, v_refs = (rf_ref, rb_ref), (kf_ref, kb_ref), (vf_ref, vb_ref)
    a_refs, lw_refs, y_refs = (af_ref, ab_ref), (lwf_ref, lwb_ref), (yf_ref, yb_ref)

    @pl.when(c == 0)
    def _():
        h_ref[...] = s0_ref[:, 0]

    row = lax.broadcasted_iota(jnp.int32, (cl, LANES), 0)
    col = lax.broadcasted_iota(jnp.int32, (cl, LANES), 1) % HEAD_DIM
    dt = row - col
    mask_s = (dt > 0, dt < 0)
    mask_i = (dt >= 0, dt <= 0)
    eye = (row == col).astype(F32)
    tri = tuple(m[:, :cl].astype(BF16) for m in mask_i)
    m0 = lax.broadcasted_iota(jnp.int32, (1, LANES), 1) < HEAD_DIM

    def diag2(x):
        return jnp.concatenate([jnp.where(m0, x, 0.0), jnp.where(m0, 0.0, x)], axis=0)

    def side_by_side_t(x):
        t = jnp.transpose(diag2(x))
        return t[:cl] + t[cl:]

    def each(f, *lists):
        return [f(*xs) for xs in zip(*lists)]

    def group_body(chains):
        r = [r_refs[dr][p, sub * cl:(sub + 1) * cl] for dr, p, sub in chains]
        k = [k_refs[dr][p, sub * cl:(sub + 1) * cl] for dr, p, sub in chains]
        v = [v_refs[dr][p, sub * cl:(sub + 1) * cl] for dr, p, sub in chains]
        a = [a_refs[dr][0, p, sub * cl:(sub + 1) * cl] for dr, p, sub in chains]
        lw = [lw_refs[dr][0, p, sub * cl:(sub + 1) * cl] for dr, p, sub in chains]
        ms = [mask_s[dr] for dr, _, _ in chains]
        mi = [mask_i[dr] for dr, _, _ in chains]
        kkr = [x * kkw_ref[p] for x, (_, p, _) in zip(k, chains)]
        kk = each(lambda x: x * lax.rsqrt(jnp.maximum(_head_sum(x * x, m0), 1e-12)), kkr)
        kd = [x * (1.0 + (y - 1.0) * ka_ref[p]) for x, y, (_, p, _) in zip(k, a, chains)]
        bv = each(lambda x, y: x * y, kk, a)

        def cumsum(x, tri_d):
            hi = x.astype(BF16)
            lo = (x - hi.astype(F32)).astype(BF16)
            cs2 = _dot(tri_d, jnp.concatenate([hi, lo], axis=1))
            return cs2[:, :LANES] + cs2[:, LANES:]

        cs = each(cumsum, lw, [tri[dr] for dr, _, _ in chains])
        tot = each(lambda x: jnp.sum(x, axis=0, keepdims=True), lw)
        e_neg = each(lambda x: jnp.exp(-x), cs)
        e_rem = each(lambda t, x: jnp.exp(t - x), tot, cs)
        rt = each(lambda x, y: x * jnp.exp(y), r, cs)
        at = each(lambda x, y, z: (-x * jnp.exp(y - z)).astype(BF16), kk, cs, lw)
        kt_d = each(lambda x, y: diag2((x * y).astype(BF16)), kd, e_neg)
        bt_d = each(lambda x, y: diag2((x * y).astype(BF16)), bv, e_neg)
        kb_t = each(lambda x, y, z: jnp.concatenate([side_by_side_t(x * z), side_by_side_t(y * z)], axis=1)
                    .astype(BF16), kd, bv, e_rem)
        v_d = each(lambda x: diag2(x.astype(BF16)), v)

        def scores(at_, rt_, bt_, kt_):
            lhs = jnp.concatenate([at_, rt_.astype(BF16)], axis=0)
            rhs = jnp.concatenate([bt_, kt_], axis=0)
            return lax.dot_general(lhs, rhs, (((1,), (1,)), ((), ())), preferred_element_type=F32)

        o1 = each(scores, at, rt, bt_d, kt_d)
        a_ab = each(lambda o, m: jnp.where(m, o[:cl, :LANES], 0.0), o1, ms)
        a_ak = each(lambda o, m: jnp.where(m, o[:cl, LANES:], 0.0).astype(BF16), o1, ms)
        a_r = each(lambda o, m: jnp.concatenate([jnp.where(m, o[cl:, LANES:], 0.0),
                                                  jnp.where(m, o[cl:, :LANES], 0.0)], axis=1).astype(BF16), o1, mi)

        t_inv = each(lambda x: eye + x, a_ab)
        npow = each(lambda x: x.astype(BF16), a_ab)
        npow = each(lambda x: _dot(x, diag2(x)).astype(BF16), npow)
        for _ in range(cl.bit_length() - 3):
            both = each(lambda n, t: _dot(n, jnp.concatenate([diag2(n), diag2(t.astype(BF16))], axis=1)),
                        npow, t_inv)
            npow = each(lambda x: x[:, :LANES].astype(BF16), both)
            t_inv = each(lambda t, x: t + x[:, LANES:], t_inv, both)
        t_inv = each(lambda t, n: t + _dot(n, diag2(t.astype(BF16))), t_inv, npow)

        x1 = each(lambda x, y: _dot(x, y).astype(BF16), a_ak, v_d)
        ut = each(lambda t, x, y: _dot(t.astype(BF16), jnp.concatenate([diag2(x), diag2(y)], axis=1))
                  .astype(BF16), t_inv, x1, at)
        rhs2 = each(lambda x, u: jnp.concatenate(
            [jnp.concatenate([x, jnp.zeros_like(x)], axis=1),
             jnp.concatenate([diag2(u[:, :LANES]), diag2(u[:, LANES:])], axis=1)], axis=0), v_d, ut)
        o2 = each(lambda x, y, z: _dot(jnp.concatenate([x, y], axis=0), z), a_r, kb_t, rhs2)

        def decay_col(t):
            gc = jnp.transpose(jnp.broadcast_to(jnp.exp(t), (LANES, LANES)))
            return jnp.where(m0, gc[:cl], gc[cl:])

        return dict(zip(chains, zip(o2, rt, each(decay_col, tot))))

    def state_stage(local, dps, step):
        subs = [step if dr == 0 else nsub - 1 - step for dr, _ in dps]
        o2, rt, dec = zip(*[local[(dr, p, sub)] for (dr, p), sub in zip(dps, subs)])
        h = [h_ref[dr, p] for dr, p in dps]
        h_d = each(lambda x: diag2(x.astype(BF16)), h)
        y = each(lambda o, rt_, x: o[:cl, :LANES] + _dot((rt_ + o[:cl, LANES:]).astype(BF16), x), o2, rt, h_d)
        h_new = each(lambda g, x, o, xd: g * x + _dot(o[cl:, LANES:].astype(BF16), xd) + o[cl:, :LANES],
                     dec, h, o2, h_d)
        for (dr, p), sub, y_p, hn in zip(dps, subs, y, h_new):
            y_refs[dr][p, sub * cl:(sub + 1) * cl] = y_p.astype(BF16)
            h_ref[dr, p] = hn

    dir_pairs = [(dr, p) for p in range(npair) for dr in range(2)]
    for g0 in range(0, len(dir_pairs), DIR_PAIR_GROUP):
        dps = dir_pairs[g0:g0 + DIR_PAIR_GROUP]
        local = group_body([(dr, p, sub) for sub in range(nsub) for dr, p in dps])
        for step in range(nsub):
            state_stage(local, dps, step)

    @pl.when(c == n_chunks - 1)
    def _():
        sfin_ref[:, 0] = h_ref[...]


def _scan(r, k, v, a, lw, kkw, ka, s0, batch):
    npair, ntok, _ = r.shape
    assert CHUNK == HEAD_DIM, "the side-by-side tiles assume chunk length == head dim"
    cl = SCAN_SUBCHUNKS * CHUNK
    n_chunks = ntok // batch // cl

    def fwd(b, c):
        return b * n_chunks + c

    def bwd(b, c):
        return b * n_chunks + n_chunks - 1 - c

    def shared(idx):
        return pl.BlockSpec((npair, cl, LANES), lambda b, c: (0, idx(b, c), 0))

    def per_dir(dr, idx):
        return pl.BlockSpec((1, npair, cl, LANES), lambda b, c: (dr, 0, idx(b, c), 0))

    state = pl.BlockSpec((2, 1, npair, HEAD_DIM, LANES), lambda b, c: (0, b, 0, 0, 0))
    y_shape = jax.ShapeDtypeStruct((npair, ntok, LANES), BF16)
    return pl.pallas_call(
        functools.partial(_scan_kernel, n_chunks),
        grid=(batch, n_chunks),
        in_specs=[shared(fwd), shared(fwd), shared(fwd), shared(bwd), shared(bwd), shared(bwd),
                  per_dir(0, fwd), per_dir(0, fwd), per_dir(1, bwd), per_dir(1, bwd),
                  _resident(kkw.shape), _resident(ka.shape), state],
        out_specs=[shared(fwd), shared(bwd), state],
        out_shape=[y_shape, y_shape, jax.ShapeDtypeStruct(s0.shape, F32)],
        scratch_shapes=[pltpu.VMEM((2, npair, HEAD_DIM, LANES), F32)],
        compiler_params=pltpu.CompilerParams(
            dimension_semantics=("parallel", "arbitrary"), vmem_limit_bytes=VMEM_LIMIT),
        name="wkv_scan",
    )(r, k, v, r, k, v, a, lw, a, lw, kkw, ka, s0)


def _post_kernel(d, yf_ref, yb_ref, gout_ref, bonus_ref, ga_ref, gbyb_ref, x_ref, mod_ref, lnw_ref, lnb_ref,
                 wpa_ref, wo_ref, g2n_ref, wff1_ref, wff2_ref, gfin_ref, o_ref, xa_ref):
    npair = d // LANES
    mod = mod_ref[0]
    gt1 = mod[2:3]
    sh2 = mod[3:4]
    sc2 = mod[4:5]
    gt2 = mod[5:6]
    m0 = lax.broadcasted_iota(jnp.int32, (1, LANES), 1) < HEAD_DIM
    inv_n = 1.0 / HEAD_DIM
    for p in range(npair):
        sl = slice(p * LANES, (p + 1) * LANES)
        wkv = yf_ref[p].astype(F32) + yb_ref[p].astype(F32)
        cen = wkv - _head_sum(wkv, m0) * inv_n
        var = _head_sum(cen * cen, m0) * inv_n
        ln = cen * lax.rsqrt(var + LNX_EPS) * lnw_ref[:, sl] + lnb_ref[:, sl]
        xa_ref[:, sl] = ((ln + bonus_ref[:, sl].astype(F32)) * gout_ref[:, sl].astype(F32)).astype(BF16)
    ya = _dot(xa_ref[...], wpa_ref[...])
    merged = ga_ref[...].astype(F32) * ya + gbyb_ref[...].astype(F32)
    x1 = x_ref[...] + gt1 * _dot(merged.astype(BF16), wo_ref[...])
    xn2 = (_rms(x1) * g2n_ref[...] * (1.0 + sc2) + sh2).astype(BF16)
    hid = jnp.maximum(_dot(xn2, wff1_ref[...]), 0.0)
    x2 = x1 + gt2 * _dot((hid * hid).astype(BF16), wff2_ref[...])
    o_ref[...] = _rms(x2) * gfin_ref[...]


def _post(yf, yb, gout, bonus, ga, gbyb, x, mod, tokens_per_mod, wts):
    ntok, d = x.shape
    tm = POST_TOKEN_TILE
    npair = d // LANES
    flat_spec = pl.BlockSpec((tm, d), lambda i: (i, 0))
    pm_spec = pl.BlockSpec((npair, tm, LANES), lambda i: (0, i, 0))
    names = ("lnx_w", "lnx_b", "w_pa", "w_o", "g2n", "w_ff1", "w_ff2", "g_fin")
    return pl.pallas_call(
        functools.partial(_post_kernel, d),
        grid=(ntok // tm,),
        in_specs=[pm_spec, pm_spec,
                  flat_spec, flat_spec, flat_spec, flat_spec, flat_spec,
                  pl.BlockSpec((1,) + mod.shape[1:], lambda i: ((i * tm) // tokens_per_mod, 0, 0))]
                 + [_resident(wts[n].shape) for n in names],
        out_specs=flat_spec,
        out_shape=jax.ShapeDtypeStruct((ntok, d), F32),
        scratch_shapes=[pltpu.VMEM((tm, d), BF16)],
        compiler_params=pltpu.CompilerParams(dimension_semantics=("parallel",), vmem_limit_bytes=VMEM_LIMIT),
        name="post_mlp",
    )(yf, yb, gout, bonus, ga, gbyb, x, mod, *[wts[n] for n in names])


def _state_to_pairs(s):
    b, h, n, _ = s.shape
    x = s.reshape(b, h // PAIR, PAIR, n, n)
    return jnp.transpose(x, (0, 1, 4, 2, 3)).reshape(b, h // PAIR, n, PAIR * n)


def _pairs_to_state(t):
    b, npair, n, _ = t.shape
    x = t.reshape(b, npair, n, PAIR, n)
    return jnp.transpose(x, (0, 1, 3, 4, 2)).reshape(b, npair * PAIR, n, n)


def _layer(x3, mod, s0_f, s0_b, seg, wts):
    batch, t, d = x3.shape
    x = x3.reshape(batch * t, d)
    tokens_per_mod = t if mod.shape[0] == batch else batch * t
    r, k, v, a, lw, gout, bonus, ga, gbyb = _in_proj(x, mod, tokens_per_mod, seg, wts)
    s0 = jnp.stack([_state_to_pairs(s0_f), _state_to_pairs(s0_b)], axis=0)
    yf, yb, sfin = _scan(r, k, v, a, lw, wts["k_k_pm"], wts["k_a_pm"], s0, batch)
    out = _post(yf, yb, gout, bonus, ga, gbyb, x, mod, tokens_per_mod, wts)
    return out.reshape(batch, t, d), _pairs_to_state(sfin[0]), _pairs_to_state(sfin[1])


def kernel(x_prompt, x_sample, state_rwkv_fwd, state_rwkv_bwd, c, c_ctx, norm1_g, norm2_g, w_ada, b_ada, w_in, w0, w2, a0, a2, g2, k_k, k_a, r_k, lnx_w, lnx_b, conv_w, w_pa, w_pb, w_o, w_ff1, w_ff2, final_norm_g):
    depth, d = norm1_g.shape
    assert depth == 1, "the fused final norm assumes a single layer"
    n_heads = r_k.shape[1]
    assert n_heads * HEAD_DIM == d and r_k.shape[2] == HEAD_DIM
    npair = d // LANES
    lora = w2.shape[2]
    assert 2 * lora == LANES and g2.shape[1] == LANES and a2.shape[2] == lora

    def blockdiag2(w):
        z = jnp.zeros_like(w[0])
        return jnp.concatenate([jnp.concatenate([w[0], z], axis=1), jnp.concatenate([z, w[1]], axis=1)], axis=0)

    wts = {
        "g1": norm1_g, "w_in": w_in[0].astype(BF16),
        "w2cat": blockdiag2(w2[0]).astype(BF16), "w0cat": w0[0].reshape(1, 2 * d),
        "a2cat": blockdiag2(a2[0]).astype(BF16), "a0cat": a0[0].reshape(1, 2 * d),
        "g2": g2[0].astype(BF16), "conv_w": conv_w[0], "w_pb": w_pb[0].astype(BF16),
        "k_a": k_a, "r_k": r_k[0].reshape(1, d),
        "k_k_pm": k_k[0].reshape(npair, 1, LANES), "k_a_pm": k_a[0].reshape(npair, 1, LANES),
        "lnx_w": lnx_w, "lnx_b": lnx_b, "w_pa": w_pa[0].astype(BF16), "w_o": w_o[0].astype(BF16),
        "g2n": norm2_g, "w_ff1": w_ff1[0].astype(BF16), "w_ff2": w_ff2[0].astype(BF16),
        "g_fin": final_norm_g.reshape(1, d),
    }

    n_lat = c.shape[0]
    rows = -(-(1 + n_lat) // 8) * 8
    cv = jnp.concatenate([c_ctx[None, :], c, jnp.zeros((rows - 1 - n_lat, d), F32)], axis=0)
    mod = _ada(cv, w_ada[0], b_ada).reshape(rows, 6, d)

    bp = x_prompt.shape[0]
    zero_state = jnp.zeros((bp, n_heads, HEAD_DIM, HEAD_DIM), F32)
    y_prompt, sf, sb = _layer(x_prompt, mod[0:1], zero_state, zero_state, x_prompt.shape[1], wts)
    y_sample, _, _ = _layer(x_sample, mod[1:1 + n_lat], state_rwkv_fwd[:, 0], state_rwkv_bwd[:, 0], GRID_W, wts)
    return (y_prompt, y_sample, sf[:, None], sb[:, None])
```

```python
import functools

import jax
import jax.numpy as jnp
from jax import lax
from jax.experimental import pallas as pl
from jax.experimental.pallas import tpu as pltpu

F32 = jnp.float32
BF16 = jnp.bfloat16

HEAD_DIM = 64
LANES = 128
PAIR = LANES // HEAD_DIM
GRID_W = 64
EPS = 1e-6
LNX_EPS = 64e-5
CHUNK = 64
TOKEN_TILE = 256
POST_TOKEN_TILE = 512
SCAN_SUBCHUNKS = 2
DIR_PAIR_GROUP = 8
VMEM_LIMIT = 58 * 1024 * 1024


def _resident(shape):
    nd = len(shape)
    return pl.BlockSpec(shape, lambda *_: (0,) * nd, pipeline_mode=pl.Buffered(1))


def _dot(a, b):
    return jnp.dot(a, b, preferred_element_type=F32)


def _softplus(x):
    return jnp.maximum(x, 0.0) + jnp.log1p(jnp.exp(-jnp.abs(x)))


def _head_sum(x, m0):
    lo = jnp.sum(jnp.where(m0, x, 0.0), axis=-1, keepdims=True)
    hi = jnp.sum(jnp.where(m0, 0.0, x), axis=-1, keepdims=True)
    return jnp.where(m0, lo, hi)


def _rms(x):
    return x * lax.rsqrt(jnp.mean(x * x, axis=-1, keepdims=True) + EPS)


def _ada_kernel(c_ref, w_ref, b_ref, o_ref):
    c = c_ref[...]
    s = (c * jax.nn.sigmoid(c)).astype(BF16)
    o_ref[...] = _dot(s, w_ref[...].astype(BF16)) + b_ref[...]


def _ada(cv, w_ada, b_ada):
    rows, d = cv.shape
    n = w_ada.shape[1]
    tn = d
    return pl.pallas_call(
        _ada_kernel,
        grid=(n // tn,),
        in_specs=[
            pl.BlockSpec((rows, d), lambda j: (0, 0)),
            pl.BlockSpec((d, tn), lambda j: (0, j)),
            pl.BlockSpec((1, tn), lambda j: (0, j)),
        ],
        out_specs=pl.BlockSpec((rows, tn), lambda j: (0, j)),
        out_shape=jax.ShapeDtypeStruct((rows, n), F32),
        name="ada_mod",
    )(cv, w_ada, b_ada)


def _in_proj_kernel(seg, d, x_ref, mod_ref, g1_ref, win_ref, w2_ref, w0_ref, a2_ref, a0_ref, g2_ref,
                    cw_ref, wpb_ref, ka_ref, rk_ref,
                    r_ref, k_ref, v_ref, a_ref, lw_ref, gout_ref, bonus_ref, ga_ref, gbyb_ref):
    tm = x_ref.shape[0]
    npair = d // LANES
    mod = mod_ref[0]
    sh1 = mod[0:1]
    sc1 = mod[1:2]
    xn = _rms(x_ref[...]) * g1_ref[...]
    xn = (xn * (1.0 + sc1) + sh1).astype(BF16)

    def proj(lo, hi):
        return _dot(xn, win_ref[:, lo:hi])

    o_lora = 3 * d
    zl = proj(o_lora, o_lora + 3 * LANES)
    wl = zl[:, 0:LANES]
    al = zl[:, LANES:2 * LANES]
    gl = zl[:, 2 * LANES:3 * LANES]
    pre_w = _dot(jnp.tanh(wl).astype(BF16), w2_ref[...]) + w0_ref[...]
    lw = -jnp.exp(-_softplus(-pre_w) - 0.5)
    a = jax.nn.sigmoid(_dot(al.astype(BF16), a2_ref[...]) + a0_ref[...])
    gout_ref[...] = _dot(jax.nn.sigmoid(gl).astype(BF16), g2_ref[...]).astype(BF16)

    r = proj(0, d)
    k = proj(d, 2 * d)
    v = proj(2 * d, 3 * d)
    q = r * rk_ref[...] * k * (2.0 + (a[:, :d] + a[:, d:] - 2.0) * ka_ref[...])
    m0 = lax.broadcasted_iota(jnp.int32, (1, LANES), 1) < HEAD_DIM
    for p in range(npair):
        sl = slice(p * LANES, (p + 1) * LANES)
        r_ref[p] = r[:, sl]
        k_ref[p] = k[:, sl]
        v_ref[p] = v[:, sl].astype(BF16)
        for dr in range(2):
            sd = slice(dr * d + p * LANES, dr * d + (p + 1) * LANES)
            a_ref[dr, p] = a[:, sd]
            lw_ref[dr, p] = lw[:, sd]
        bonus_ref[:, sl] = (v[:, sl] * _head_sum(q[:, sl], m0)).astype(BF16)

    o_conv = o_lora + 3 * LANES
    cb = proj(o_conv, o_conv + d)
    u = proj(o_conv + d, o_conv + 2 * d) * proj(o_conv + 2 * d, o_conv + 3 * d)
    pos = lax.broadcasted_iota(jnp.int32, (tm, 1), 0) % seg
    u_prev = jnp.where(pos == 0, 0.0, pltpu.roll(u, 1, 0))
    u_next = jnp.where(pos == seg - 1, 0.0, pltpu.roll(u, tm - 1, 0))
    conv = cw_ref[0:1] * u_prev + cw_ref[1:2] * u + cw_ref[2:3] * u_next
    yb = _dot((cb * conv).astype(BF16), wpb_ref[...])

    o_gate = o_conv + 3 * d
    ga_ref[...] = jax.nn.sigmoid(proj(o_gate, o_gate + d)).astype(BF16)
    gbyb_ref[...] = (jax.nn.sigmoid(proj(o_gate + d, o_gate + 2 * d)) * yb).astype(BF16)


def _in_proj(x, mod, tokens_per_mod, seg, wts):
    ntok, d = x.shape
    tm = TOKEN_TILE
    npair = d // LANES
    flat = jax.ShapeDtypeStruct((ntok, d), F32)
    pm = jax.ShapeDtypeStruct((npair, ntok, LANES), F32)
    pm2 = jax.ShapeDtypeStruct((2, npair, ntok, LANES), F32)
    flat_b = jax.ShapeDtypeStruct((ntok, d), BF16)
    pm_b = jax.ShapeDtypeStruct((npair, ntok, LANES), BF16)
    flat_spec = pl.BlockSpec((tm, d), lambda i: (i, 0))
    pm_spec = pl.BlockSpec((npair, tm, LANES), lambda i: (0, i, 0))
    pm2_spec = pl.BlockSpec((2, npair, tm, LANES), lambda i: (0, 0, i, 0))
    names = ("g1", "w_in", "w2cat", "w0cat", "a2cat", "a0cat", "g2", "conv_w", "w_pb", "k_a", "r_k")
    return pl.pallas_call(
        functools.partial(_in_proj_kernel, seg, d),
        grid=(ntok // tm,),
        in_specs=[flat_spec,
                  pl.BlockSpec((1,) + mod.shape[1:], lambda i: ((i * tm) // tokens_per_mod, 0, 0))]
                 + [_resident(wts[n].shape) for n in names],
        out_specs=[pm_spec, pm_spec, pm_spec, pm2_spec, pm2_spec, flat_spec, flat_spec, flat_spec, flat_spec],
        out_shape=[pm, pm, pm_b, pm2, pm2, flat_b, flat_b, flat_b, flat_b],
        compiler_params=pltpu.CompilerParams(dimension_semantics=("parallel",), vmem_limit_bytes=VMEM_LIMIT),
        name="in_proj",
    )(x, mod, *[wts[n] for n in names])


def _scan_kernel(n_chunks, rf_ref, kf_ref, vf_ref, rb_ref, kb_ref, vb_ref, af_ref, lwf_ref, ab_ref, lwb_ref,
                 kkw_ref, ka_ref, s0_ref, yf_ref, yb_ref, sfin_ref, h_ref):
    c = pl.program_id(1)
    npair = rf_ref.shape[0]
    cl = CHUNK
    nsub = rf_ref.shape[1] // cl
    r_refs, k_refs, v_refs = (rf_ref, rb_ref), (kf_ref, kb_ref), (vf_ref, vb_ref)
    a_refs, lw_refs, y_refs = (af_ref, ab_ref), (lwf_ref, lwb_ref), (yf_ref, yb_ref)

    @pl.when(c == 0)
    def _():
        h_ref[...] = s0_ref[:, 0]

    row = lax.broadcasted_iota(jnp.int32, (cl, LANES), 0)
    col = lax.broadcasted_iota(jnp.int32, (cl, LANES), 1) % HEAD_DIM
    dt = row - col
    mask_s = (dt > 0, dt < 0)
    mask_i = (dt >= 0, dt <= 0)
    eye = (row == col).astype(F32)
    tri = tuple(m[:, :cl].astype(BF16) for m in mask_i)
    m0 = lax.broadcasted_iota(jnp.int32, (1, LANES), 1) < HEAD_DIM

    def diag2(x):
        return jnp.concatenate([jnp.where(m0, x, 0.0), jnp.where(m0, 0.0, x)], axis=0)

    def side_by_side_t(x):
        t = jnp.transpose(diag2(x))
        return t[:cl] + t[cl:]

    def each(f, *lists):
        return [f(*xs) for xs in zip(*lists)]

    def group_body(chains):
        r = [r_refs[dr][p, sub * cl:(sub + 1) * cl] for dr, p, sub in chains]
        k = [k_refs[dr][p, sub * cl:(sub + 1) * cl] for dr, p, sub in chains]
        v = [v_refs[dr][p, sub * cl:(sub + 1) * cl] for dr, p, sub in chains]
        a = [a_refs[dr][0, p, sub * cl:(sub + 1) * cl] for dr, p, sub in chains]
        lw = [lw_refs[dr][0, p, sub * cl:(sub + 1) * cl] for dr, p, sub in chains]
        ms = [mask_s[dr] for dr, _, _ in chains]
        mi = [mask_i[dr] for dr, _, _ in chains]
        kkr = [x * kkw_ref[p] for x, (_, p, _) in zip(k, chains)]
        kk = each(lambda x: x * lax.rsqrt(jnp.maximum(_head_sum(x * x, m0), 1e-12)), kkr)
        kd = [x * (1.0 + (y - 1.0) * ka_ref[p]) for x, y, (_, p, _) in zip(k, a, chains)]
        bv = each(lambda x, y: x * y, kk, a)

        def cumsum(x, tri_d):
            hi = x.astype(BF16)
            lo = (x - hi.astype(F32)).astype(BF16)
            cs2 = _dot(tri_d, jnp.concatenate([hi, lo], axis=1))
            return cs2[:, :LANES] + cs2[:, LANES:]

        cs = each(cumsum, lw, [tri[dr] for dr, _, _ in chains])
        tot = each(lambda x: jnp.sum(x, axis=0, keepdims=True), lw)
        e_neg = each(lambda x: jnp.exp(-x), cs)
        e_rem = each(lambda t, x: jnp.exp(t - x), tot, cs)
        rt = each(lambda x, y: x * jnp.exp(y), r, cs)
        at = each(lambda x, y, z: (-x * jnp.exp(y - z)).astype(BF16), kk, cs, lw)
        kt_d = each(lambda x, y: diag2((x * y).astype(BF16)), kd, e_neg)
        bt_d = each(lambda x, y: diag2((x * y).astype(BF16)), bv, e_neg)
        kb_t = each(lambda x, y, z: jnp.concatenate([side_by_side_t(x * z), side_by_side_t(y * z)], axis=1)
                    .astype(BF16), kd, bv, e_rem)
        v_d = each(lambda x: diag2(x.astype(BF16)), v)

        def scores(at_, rt_, bt_, kt_):
            lhs = jnp.concatenate([at_, rt_.astype(BF16)], axis=0)
            rhs = jnp.concatenate([bt_, kt_], axis=0)
            return lax.dot_general(lhs, rhs, (((1,), (1,)), ((), ())), preferred_element_type=F32)

        o1 = each(scores, at, rt, bt_d, kt_d)
        a_ab = each(lambda o, m: jnp.where(m, o[:cl, :LANES], 0.0), o1, ms)
        a_ak = each(lambda o, m: jnp.where(m, o[:cl, LANES:], 0.0).astype(BF16), o1, ms)
        a_r = each(lambda o, m: jnp.concatenate([jnp.where(m, o[cl:, LANES:], 0.0),
                                                  jnp.where(m, o[cl:, :LANES], 0.0)], axis=1).astype(BF16), o1, mi)

        t_inv = each(lambda x: eye + x, a_ab)
        npow = each(lambda x: x.astype(BF16), a_ab)
        npow = each(lambda x: _dot(x, diag2(x)).astype(BF16), npow)
        for _ in range(cl.bit_length() - 3):
            both = each(lambda n, t: _dot(n, jnp.concatenate([diag2(n), diag2(t.astype(BF16))], axis=1)),
                        npow, t_inv)
            npow = each(lambda x: x[:, :LANES].astype(BF16), both)
            t_inv = each(lambda t, x: t + x[:, LANES:], t_inv, both)
        t_inv = each(lambda t, n: t + _dot(n, diag2(t.astype(BF16))), t_inv, npow)

        x1 = each(lambda x, y: _dot(x, y).astype(BF16), a_ak, v_d)
        ut = each(lambda t, x, y: _dot(t.astype(BF16), jnp.concatenate([diag2(x), diag2(y)], axis=1))
                  .astype(BF16), t_inv, x1, at)
        rhs2 = each(lambda x, u: jnp.concatenate(
            [jnp.concatenate([x, jnp.zeros_like(x)], axis=1),
             jnp.concatenate([diag2(u[:, :LANES]), diag2(u[:, LANES:])], axis=1)], axis=0), v_d, ut)
        o2 = each(lambda x, y, z: _dot(jnp.concatenate([x, y], axis=0), z), a_r, kb_t, rhs2)

        def decay_col(t):
            gc = jnp.transpose(jnp.broadcast_to(jnp.exp(t), (LANES, LANES)))
            return jnp.where(m0, gc[:cl], gc[cl:])

        return dict(zip(chains, zip(o2, rt, each(decay_col, tot))))

    def state_stage(local, dps, step):
        subs = [step if dr == 0 else nsub - 1 - step for dr, _ in dps]
        o2, rt, dec = zip(*[local[(dr, p, sub)] for (dr, p), sub in zip(dps, subs)])
        h = [h_ref[dr, p] for dr, p in dps]
        h_d = each(lambda x: diag2(x.astype(BF16)), h)
        y = each(lambda o, rt_, x: o[:cl, :LANES] + _dot((rt_ + o[:cl, LANES:]).astype(BF16), x), o2, rt, h_d)
        h_new = each(lambda g, x, o, xd: g * x + _dot(o[cl:, LANES:].astype(BF16), xd) + o[cl:, :LANES],
                     dec, h, o2, h_d)
        for (dr, p), sub, y_p, hn in zip(dps, subs, y, h_new):
            y_refs[dr][p, sub * cl:(sub + 1) * cl] = y_p.astype(BF16)
            h_ref[dr, p] = hn

    dir_pairs = [(dr, p) for p in range(npair) for dr in range(2)]
    for g0 in range(0, len(dir_pairs), DIR_PAIR_GROUP):
        dps = dir_pairs[g0:g0 + DIR_PAIR_GROUP]
        local = group_body([(dr, p, sub) for sub in range(nsub) for dr, p in dps])
        for step in range(nsub):
            state_stage(local, dps, step)

    @pl.when(c == n_chunks - 1)
    def _():
        sfin_ref[:, 0] = h_ref[...]


def _scan(r, k, v, a, lw, kkw, ka, s0, batch):
    npair, ntok, _ = r.shape
    assert CHUNK == HEAD_DIM, "the side-by-side tiles assume chunk length == head dim"
    cl = SCAN_SUBCHUNKS * CHUNK
    n_chunks = ntok // batch // cl

    def fwd(b, c):
        return b * n_chunks + c

    def bwd(b, c):
        return b * n_chunks + n_chunks - 1 - c

    def shared(idx):
        return pl.BlockSpec((npair, cl, LANES), lambda b, c: (0, idx(b, c), 0))

    def per_dir(dr, idx):
        return pl.BlockSpec((1, npair, cl, LANES), lambda b, c: (dr, 0, idx(b, c), 0))

    state = pl.BlockSpec((2, 1, npair, HEAD_DIM, LANES), lambda b, c: (0, b, 0, 0, 0))
    y_shape = jax.ShapeDtypeStruct((npair, ntok, LANES), BF16)
    return pl.pallas_call(
        functools.partial(_scan_kernel, n_chunks),
        grid=(batch, n_chunks),
        in_specs=[shared(fwd), shared(fwd), shared(fwd), shared(bwd), shared(bwd), shared(bwd),
                  per_dir(0, fwd), per_dir(0, fwd), per_dir(1, bwd), per_dir(1, bwd),
                  _resident(kkw.shape), _resident(ka.shape), state],
        out_specs=[shared(fwd), shared(bwd), state],
        out_shape=[y_shape, y_shape, jax.ShapeDtypeStruct(s0.shape, F32)],
        scratch_shapes=[pltpu.VMEM((2, npair, HEAD_DIM, LANES), F32)],
        compiler_params=pltpu.CompilerParams(
            dimension_semantics=("parallel", "arbitrary"), vmem_limit_bytes=VMEM_LIMIT),
        name="wkv_scan",
    )(r, k, v, r, k, v, a, lw, a, lw, kkw, ka, s0)


def _post_kernel(d, yf_ref, yb_ref, gout_ref, bonus_ref, ga_ref, gbyb_ref, x_ref, mod_ref, lnw_ref, lnb_ref,
                 wpa_ref, wo_ref, g2n_ref, wff1_ref, wff2_ref, gfin_ref, o_ref, xa_ref):
    npair = d // LANES
    mod = mod_ref[0]
    gt1 = mod[2:3]
    sh2 = mod[3:4]
    sc2 = mod[4:5]
    gt2 = mod[5:6]
    m0 = lax.broadcasted_iota(jnp.int32, (1, LANES), 1) < HEAD_DIM
    inv_n = 1.0 / HEAD_DIM
    for p in range(npair):
        sl = slice(p * LANES, (p + 1) * LANES)
        wkv = yf_ref[p].astype(F32) + yb_ref[p].astype(F32)
        cen = wkv - _head_sum(wkv, m0) * inv_n
        var = _head_sum(cen * cen, m0) * inv_n
        ln = cen * lax.rsqrt(var + LNX_EPS) * lnw_ref[:, sl] + lnb_ref[:, sl]
        xa_ref[:, sl] = ((ln + bonus_ref[:, sl].astype(F32)) * gout_ref[:, sl].astype(F32)).astype(BF16)
    ya = _dot(xa_ref[...], wpa_ref[...])
    merged = ga_ref[...].astype(F32) * ya + gbyb_ref[...].astype(F32)
    x1 = x_ref[...] + gt1 * _dot(merged.astype(BF16), wo_ref[...])
    xn2 = (_rms(x1) * g2n_ref[...] * (1.0 + sc2) + sh2).astype(BF16)
    hid = jnp.maximum(_dot(xn2, wff1_ref[...]), 0.0)
    x2 = x1 + gt2 * _dot((hid * hid).astype(BF16), wff2_ref[...])
    o_ref[...] = _rms(x2) * gfin_ref[...]


def _post(yf, yb, gout, bonus, ga, gbyb, x, mod, tokens_per_mod, wts):
    ntok, d = x.shape
    tm = POST_TOKEN_TILE
    npair = d // LANES
    flat_spec = pl.BlockSpec((tm, d), lambda i: (i, 0))
    pm_spec = pl.BlockSpec((npair, tm, LANES), lambda i: (0, i, 0))
    names = ("lnx_w", "lnx_b", "w_pa", "w_o", "g2n", "w_ff1", "w_ff2", "g_fin")
    return pl.pallas_call(
        functools.partial(_post_kernel, d),
        grid=(ntok // tm,),
        in_specs=[pm_spec, pm_spec,
                  flat_spec, flat_spec, flat_spec, flat_spec, flat_spec,
                  pl.BlockSpec((1,) + mod.shape[1:], lambda i: ((i * tm) // tokens_per_mod, 0, 0))]
                 + [_resident(wts[n].shape) for n in names],
        out_specs=flat_spec,
        out_shape=jax.ShapeDtypeStruct((ntok, d), F32),
        scratch_shapes=[pltpu.VMEM((tm, d), BF16)],
        compiler_params=pltpu.CompilerParams(dimension_semantics=("parallel",), vmem_limit_bytes=VMEM_LIMIT),
        name="post_mlp",
    )(yf, yb, gout, bonus, ga, gbyb, x, mod, *[wts[n] for n in names])


def _state_to_pairs(s):
    b, h, n, _ = s.shape
    x = s.reshape(b, h // PAIR, PAIR, n, n)
    return jnp.transpose(x, (0, 1, 4, 2, 3)).reshape(b, h // PAIR, n, PAIR * n)


def _pairs_to_state(t):
    b, npair, n, _ = t.shape
    x = t.reshape(b, npair, n, PAIR, n)
    return jnp.transpose(x, (0, 1, 3, 4, 2)).reshape(b, npair * PAIR, n, n)


def _layer(x3, mod, s0_f, s0_b, seg, wts):
    batch, t, d = x3.shape
    x = x3.reshape(batch * t, d)
    tokens_per_mod = t if mod.shape[0] == batch else batch * t
    r, k, v, a, lw, gout, bonus, ga, gbyb = _in_proj(x, mod, tokens_per_mod, seg, wts)
    s0 = jnp.stack([_state_to_pairs(s0_f), _state_to_pairs(s0_b)], axis=0)
    yf, yb, sfin = _scan(r, k, v, a, lw, wts["k_k_pm"], wts["k_a_pm"], s0, batch)
    out = _post(yf, yb, gout, bonus, ga, gbyb, x, mod, tokens_per_mod, wts)
    return out.reshape(batch, t, d), _pairs_to_state(sfin[0]), _pairs_to_state(sfin[1])


def kernel(x_prompt, x_sample, state_rwkv_fwd, state_rwkv_bwd, c, c_ctx, norm1_g, norm2_g, w_ada, b_ada, w_in, w0, w2, a0, a2, g2, k_k, k_a, r_k, lnx_w, lnx_b, conv_w, w_pa, w_pb, w_o, w_ff1, w_ff2, final_norm_g):
    depth, d = norm1_g.shape
    assert depth == 1, "the fused final norm assumes a single layer"
    n_heads = r_k.shape[1]
    assert n_heads * HEAD_DIM == d and r_k.shape[2] == HEAD_DIM
    npair = d // LANES
    lora = w2.shape[2]
    assert 2 * lora == LANES and g2.shape[1] == LANES and a2.shape[2] == lora

    def blockdiag2(w):
        z = jnp.zeros_like(w[0])
        return jnp.concatenate([jnp.concatenate([w[0], z], axis=1), jnp.concatenate([z, w[1]], axis=1)], axis=0)

    wts = {
        "g1": norm1_g, "w_in": w_in[0].astype(BF16),
        "w2cat": blockdiag2(w2[0]).astype(BF16), "w0cat": w0[0].reshape(1, 2 * d),
        "a2cat": blockdiag2(a2[0]).astype(BF16), "a0cat": a0[0].reshape(1, 2 * d),
        "g2": g2[0].astype(BF16), "conv_w": conv_w[0], "w_pb": w_pb[0].astype(BF16),
        "k_a": k_a, "r_k": r_k[0].reshape(1, d),
        "k_k_pm": k_k[0].reshape(npair, 1, LANES), "k_a_pm": k_a[0].reshape(npair, 1, LANES),
        "lnx_w": lnx_w, "lnx_b": lnx_b, "w_pa": w_pa[0].astype(BF16), "w_o": w_o[0].astype(BF16),
        "g2n": norm2_g, "w_ff1": w_ff1[0].astype(BF16), "w_ff2": w_ff2[0].astype(BF16),
        "g_fin": final_norm_g.reshape(1, d),
    }

    n_lat = c.shape[0]
    rows = -(-(1 + n_lat) // 8) * 8
    cv = jnp.concatenate([c_ctx[None, :], c, jnp.zeros((rows - 1 - n_lat, d), F32)], axis=0)
    mod = _ada(cv, w_ada[0], b_ada).reshape(rows, 6, d)

    bp = x_prompt.shape[0]
    zero_state = jnp.zeros((bp, n_heads, HEAD_DIM, HEAD_DIM), F32)
    y_prompt, sf, sb = _layer(x_prompt, mod[0:1], zero_state, zero_state, x_prompt.shape[1], wts)
    y_sample, _, _ = _layer(x_sample, mod[1:1 + n_lat], state_rwkv_fwd[:, 0], state_rwkv_bwd[:, 0], GRID_W, wts)
    return (y_prompt, y_sample, sf[:, None], sb[:, None])
```
